```python
import math
import jax, jax.numpy as jnp
from jax import lax
import numpy as np

D_MODEL = 1024
BATCH = 2
SEQ = 8192
DEPTH = 4
DEC_BATCH = 32
DEC_SEQ = 4
PAST_LEN = 8192
PAGE_SIZE = 128

MIX_WIDTH = D_MODEL
GLA_HEADS = 4
GLA_DV = MIX_WIDTH // 2 // GLA_HEADS
GLA_DK = GLA_DV // 2
GLA_RANK = 16
GLA_TAU = 16.0
GLA_CHUNK = 64
DIFF_HEADS = 4
DIFF_DH = MIX_WIDTH // 2 // DIFF_HEADS // 2
DIFF_DV = 2 * DIFF_DH
Q_BLOCK = 128
ROPE_THETA = 10000.0
MEM_LEN = 256
MEM_HEADS = 4
MEM_DH = D_MODEL // MEM_HEADS
D_FF = ((8 * D_MODEL // 3 + 127) // 128) * 128
CONV_W = 3
LN_EPS = 1e-5
DEEPNORM_ALPHA = (2 * DEPTH) ** 0.25
DEEPNORM_BETA = (8 * DEPTH) ** -0.25

GLA_QK = GLA_HEADS * GLA_DK
GLA_V = GLA_HEADS * GLA_DV
DIFF_QK = DIFF_HEADS * 2 * DIFF_DH
DIFF_V = DIFF_HEADS * DIFF_DV
IN_SPLITS = (GLA_QK, GLA_QK, GLA_V, GLA_V, GLA_RANK, DIFF_QK, DIFF_QK, DIFF_V)
N_IN = 2 * GLA_QK + 2 * GLA_V + GLA_RANK + 2 * DIFF_QK + DIFF_V

kernel_name = 'hymba_gla_diffattn_convffn_decode_step'

F32 = jnp.float32


def rms_norm(x, w):
    x = x.astype(F32)
    return x * lax.rsqrt(jnp.mean(x * x, axis=-1, keepdims=True) + LN_EPS) * w.astype(F32)


def post_ln(res, sub, g, b):
    h = DEEPNORM_ALPHA * res.astype(F32) + sub.astype(F32)
    mu = jnp.mean(h, axis=-1, keepdims=True)
    hc = h - mu
    var = jnp.mean(hc * hc, axis=-1, keepdims=True)
    return (hc * lax.rsqrt(var + LN_EPS) * g.astype(F32) + b.astype(F32)).astype(res.dtype)


def rope(x, pos):
    half = DIFF_DH // 2
    inv = ROPE_THETA ** (-jnp.arange(half, dtype=F32) / half)
    ang = pos.astype(F32)[:, None] * inv[None, :]
    cos = jnp.cos(ang)[None, :, None, None, :]
    sin = jnp.sin(ang)[None, :, None, None, :]
    xf = x.astype(F32)
    x1, x2 = xf[..., :half], xf[..., half:]
    return jnp.concatenate([x1 * cos - x2 * sin, x2 * cos + x1 * sin], axis=-1).astype(x.dtype)


def mixer_inputs(x, w_in, gla_wa2, gla_ba, pos):
    B, T, _ = x.shape
    h = jnp.einsum('btd,dn->btn', x, w_in)
    offs = np.cumsum(IN_SPLITS)[:-1].tolist()
    gq, gk, gv, gr, ga, dq, dk, dv = jnp.split(h, offs, axis=-1)
    gq = gq.reshape(B, T, GLA_HEADS, GLA_DK) * (GLA_DK ** -0.5)
    gk = gk.reshape(B, T, GLA_HEADS, GLA_DK)
    gv = gv.reshape(B, T, GLA_HEADS, GLA_DV)
    loga = (jax.nn.log_sigmoid((ga @ gla_wa2 + gla_ba).astype(F32)) / GLA_TAU).reshape(B, T, GLA_HEADS, GLA_DK)
    dq = rope(dq.reshape(B, T, DIFF_HEADS, 2, DIFF_DH), pos)
    dk = rope(dk.reshape(B, T, DIFF_HEADS, 2, DIFF_DH), pos)
    dv = dv.reshape(B, T, DIFF_HEADS, DIFF_DV)
    return gq, gk, gv, gr, loga, dq, dk, dv


def gla_recurrent(q, k, v, loga, s0):
    B, T, H, DK = q.shape
    DV = v.shape[-1]
    C = math.gcd(T, GLA_CHUNK)
    n = T // C

    def to_chunks(a):
        return jnp.swapaxes(a.reshape(B, n, C, H, a.shape[-1]), 0, 1)

    tril = jnp.tril(jnp.ones((C, C), dtype=bool))

    def step(S, inp):
        qc, kc, vc, gc = (a.astype(F32) for a in inp)
        b = jnp.cumsum(gc, axis=1)
        o_inter = jnp.einsum('bchk,bhkv->bchv', qc * jnp.exp(b), S)
        rel = jnp.exp(jnp.where(tril[None, :, :, None, None], b[:, :, None] - b[:, None, :], -jnp.inf))
        att = jnp.einsum('bthk,bshk,btshk->bhts', qc, kc, rel)
        o_intra = jnp.einsum('bhts,bshv->bthv', att, vc)
        b_last = b[:, -1]
        S_new = jnp.exp(b_last)[..., None] * S + jnp.einsum('bshk,bshv->bhkv', kc * jnp.exp(b_last[:, None] - b), vc)
        return S_new, o_inter + o_intra

    S, o = lax.scan(step, s0.astype(F32), (to_chunks(q), to_chunks(k), to_chunks(v), to_chunks(loga)))
    o = jnp.swapaxes(o, 0, 1).reshape(B, T, H, DV)
    return o, S.astype(s0.dtype)


def diff_lambda_value(lp, lam_init):
    lf = lp.astype(F32)
    return jnp.exp(jnp.sum(lf[0] * lf[1])) - jnp.exp(jnp.sum(lf[2] * lf[3])) + lam_init


def diff_core(q, ks, vs, masks, lam):
    scale = DIFF_DH ** -0.5
    scores = []
    for kk, m in zip(ks, masks):
        s = jnp.einsum('bqhcd,bkhcd->bhcqk', q, kk, preferred_element_type=F32) * scale
        if m is not None:
            s = jnp.where(m, s, -jnp.inf)
        scores.append(s)
    p = jax.nn.softmax(jnp.concatenate(scores, axis=-1), axis=-1)
    a = p[:, :, 0] - lam * p[:, :, 1]
    out = None
    start = 0
    for vv in vs:
        n = vv.shape[1]
        term = jnp.einsum('bhqk,bkhe->bqhe', a[..., start:start + n].astype(vv.dtype), vv, preferred_element_type=F32)
        out = term if out is None else out + term
        start += n
    return out


def diff_attn_prompt(q, k, v, lam):
    B, T = q.shape[:2]
    QB = math.gcd(T, Q_BLOCK)
    nb = T // QB
    qb = jnp.swapaxes(q.reshape(B, nb, QB, DIFF_HEADS, 2, DIFF_DH), 0, 1)
    kpos = jnp.arange(T)

    def block(args):
        i, qi = args
        mask = (i * QB + jnp.arange(QB))[:, None] >= kpos[None, :]
        return diff_core(qi, (k,), (v,), (mask,), lam)

    o = lax.map(block, (jnp.arange(nb), qb))
    return jnp.swapaxes(o, 0, 1).reshape(B, T, DIFF_HEADS, DIFF_DV)


def mixer_output(og, gr, od, gla_norm_w, diff_norm_w, lam_init, w_out, dtype):
    B, T = og.shape[:2]
    g = rms_norm(og, gla_norm_w) * jax.nn.silu(gr.astype(F32)).reshape(B, T, GLA_HEADS, GLA_DV)
    d = rms_norm(od, diff_norm_w) * (1.0 - lam_init)
    o = jnp.concatenate([g.reshape(B, T, GLA_V), d.reshape(B, T, DIFF_V)], axis=-1).astype(dtype)
    return o @ w_out


def mem_kv(mem, wk, wv):
    B, M, _ = mem.shape
    return (mem @ wk).reshape(B, M, MEM_HEADS, MEM_DH), (mem @ wv).reshape(B, M, MEM_HEADS, MEM_DH)


def cross_attn(x, mk, mv, wq, wo):
    B, T, _ = x.shape
    q = (x @ wq).reshape(B, T, MEM_HEADS, MEM_DH)
    s = jnp.einsum('bqhd,bkhd->bhqk', q, mk, preferred_element_type=F32) * (MEM_DH ** -0.5)
    p = jax.nn.softmax(s, axis=-1)
    o = jnp.einsum('bhqk,bkhd->bqhd', p.astype(mv.dtype), mv, preferred_element_type=F32)
    return o.reshape(B, T, D_MODEL).astype(x.dtype) @ wo


def conv_ffn(x, buf, w_up, conv_w, conv_b, w_down):
    T = x.shape[1]
    u = x @ w_up
    padded = jnp.concatenate([buf.astype(u.dtype), u], axis=1)
    c = conv_b
    for j in range(CONV_W):
        c = c + conv_w[j] * padded[:, j:j + T]
    a, b = jnp.split(c, 2, axis=-1)
    h = jax.nn.silu(a) * b
    return h @ w_down, padded[:, -(CONV_W - 1):]


def setup_inputs(seed: int = 0) -> dict:
    key = jax.random.key(seed)
    ks = jax.random.split(key, 40)
    n_pages = PAST_LEN // PAGE_SIZE
    n_phys = (5 * DEC_BATCH * n_pages + 3) // 4

    def nrm(k, shape, scale):
        return jax.random.normal(k, shape, F32) * scale

    perm = jax.random.permutation(ks[0], n_phys)
    page_table = perm[:DEC_BATCH * n_pages].reshape(DEC_BATCH, n_pages).astype(jnp.int32)
    L = DEPTH
    return {
        'x_prompt': nrm(ks[1], (BATCH, SEQ, D_MODEL), 1.0),
        'x_sample': nrm(ks[2], (DEC_BATCH, DEC_SEQ, D_MODEL), 1.0),
        'mem_prompt': nrm(ks[3], (BATCH, MEM_LEN, D_MODEL), 1.0),
        'cache_k': nrm(ks[4], (L, n_phys, PAGE_SIZE, DIFF_HEADS, 2 * DIFF_DH), 1.0),
        'cache_v': nrm(ks[5], (L, n_phys, PAGE_SIZE, DIFF_HEADS, DIFF_DV), 1.0),
        'page_table': page_table,
        'cache_mem_k': nrm(ks[6], (L, DEC_BATCH, MEM_LEN, MEM_HEADS, MEM_DH), 1.0),
        'cache_mem_v': nrm(ks[7], (L, DEC_BATCH, MEM_LEN, MEM_HEADS, MEM_DH), 1.0),
        'state_gla': nrm(ks[8], (L, DEC_BATCH, GLA_HEADS, GLA_DK, GLA_DV), 1.0),
        'state_conv': nrm(ks[9], (L, DEC_BATCH, CONV_W - 1, 2 * D_FF), 1.0),
        'w_in': nrm(ks[10], (L, D_MODEL, N_IN), D_MODEL ** -0.5),
        'gla_wa2': nrm(ks[11], (L, GLA_RANK, GLA_QK), GLA_RANK ** -0.5),
        'gla_ba': nrm(ks[12], (L, GLA_QK), 0.01),
        'gla_norm_w': 1.0 + nrm(ks[13], (L, GLA_DV), 0.01),
        'diff_lambda': nrm(ks[14], (L, 4, DIFF_DH), 0.1),
        'diff_norm_w': 1.0 + nrm(ks[15], (L, DIFF_DV), 0.01),
        'w_out': nrm(ks[16], (L, MIX_WIDTH, D_MODEL), MIX_WIDTH ** -0.5 * DEEPNORM_BETA),
        'ln1_g': 1.0 + nrm(ks[17], (L, D_MODEL), 0.01),
        'ln1_b': nrm(ks[18], (L, D_MODEL), 0.01),
        'cross_wq': nrm(ks[19], (L, D_MODEL, D_MODEL), D_MODEL ** -0.5),
        'cross_wk': nrm(ks[20], (L, D_MODEL, D_MODEL), D_MODEL ** -0.5),
        'cross_wv': nrm(ks[21], (L, D_MODEL, D_MODEL), D_MODEL ** -0.5),
        'cross_wo': nrm(ks[22], (L, D_MODEL, D_MODEL), D_MODEL ** -0.5 * DEEPNORM_BETA),
        'ln2_g': 1.0 + nrm(ks[23], (L, D_MODEL), 0.01),
        'ln2_b': nrm(ks[24], (L, D_MODEL), 0.01),
        'ffn_w_up': nrm(ks[25], (L, D_MODEL, 2 * D_FF), D_MODEL ** -0.5),
        'ffn_conv_w': nrm(ks[26], (L, CONV_W, 2 * D_FF), CONV_W ** -0.5),
        'ffn_conv_b': nrm(ks[27], (L, 2 * D_FF), 0.01),
        'ffn_w_down': nrm(ks[28], (L, D_FF, D_MODEL), D_FF ** -0.5 * DEEPNORM_BETA),
        'ln3_g': 1.0 + nrm(ks[29], (L, D_MODEL), 0.01),
        'ln3_b': nrm(ks[30], (L, D_MODEL), 0.01),
    }


def reference(x_prompt, x_sample, mem_prompt, cache_k, cache_v, page_table, cache_mem_k, cache_mem_v,
              state_gla, state_conv, w_in, gla_wa2, gla_ba, gla_norm_w, diff_lambda, diff_norm_w, w_out,
              ln1_g, ln1_b, cross_wq, cross_wk, cross_wv, cross_wo, ln2_g, ln2_b,
              ffn_w_up, ffn_conv_w, ffn_conv_b, ffn_w_down, ln3_g, ln3_b):
    Bp, Tp, _ = x_prompt.shape
    Bs, Ts, _ = x_sample.shape
    n_pages = page_table.shape[1]
    past_len = n_pages * PAGE_SIZE
    pos_p = jnp.arange(Tp)
    pos_s = past_len + jnp.arange(Ts)
    causal_s = jnp.tril(jnp.ones((Ts, Ts), dtype=bool))
    xp, xs = x_prompt, x_sample
    kp_l, vp_l, ks_l, vs_l, gp_l, gs_l, cp_l, cs_l, mkp_l, mvp_l = ([] for _ in range(10))
    for l in range(DEPTH):
        lam_init = 0.8 - 0.6 * math.exp(-0.3 * l)
        lam = diff_lambda_value(diff_lambda[l], lam_init)

        gq, gk, gv, gr, ga, dq, dk, dv = mixer_inputs(xp, w_in[l], gla_wa2[l], gla_ba[l], pos_p)
        og, s_p = gla_recurrent(gq, gk, gv, ga, jnp.zeros((Bp, GLA_HEADS, GLA_DK, GLA_DV), xp.dtype))
        od = diff_attn_prompt(dq, dk, dv, lam)
        xp = post_ln(xp, mixer_output(og, gr, od, gla_norm_w[l], diff_norm_w[l], lam_init, w_out[l], xp.dtype), ln1_g[l], ln1_b[l])
        mk_p, mv_p = mem_kv(mem_prompt, cross_wk[l], cross_wv[l])
        xp = post_ln(xp, cross_attn(xp, mk_p, mv_p, cross_wq[l], cross_wo[l]), ln2_g[l], ln2_b[l])
        f_p, conv_p = conv_ffn(xp, jnp.zeros((Bp, CONV_W - 1, 2 * D_FF), xp.dtype), ffn_w_up[l], ffn_conv_w[l], ffn_conv_b[l], ffn_w_down[l])
        xp = post_ln(xp, f_p, ln3_g[l], ln3_b[l])
        kp_l.append(dk.reshape(Bp, Tp // PAGE_SIZE, PAGE_SIZE, DIFF_HEADS, 2 * DIFF_DH))
        vp_l.append(dv.reshape(Bp, Tp // PAGE_SIZE, PAGE_SIZE, DIFF_HEADS, DIFF_DV))
        gp_l.append(s_p)
        cp_l.append(conv_p)
        mkp_l.append(mk_p)
        mvp_l.append(mv_p)

        gq, gk, gv, gr, ga, dq, dk, dv = mixer_inputs(xs, w_in[l], gla_wa2[l], gla_ba[l], pos_s)
        og, s_s = gla_recurrent(gq, gk, gv, ga, state_gla[l])
        k_past = cache_k[l, page_table].reshape(Bs, past_len, DIFF_HEADS, 2, DIFF_DH)
        v_past = cache_v[l, page_table].reshape(Bs, past_len, DIFF_HEADS, DIFF_DV)
        od = diff_core(dq, (k_past, dk), (v_past, dv), (None, causal_s), lam)
        xs = post_ln(xs, mixer_output(og, gr, od, gla_norm_w[l], diff_norm_w[l], lam_init, w_out[l], xs.dtype), ln1_g[l], ln1_b[l])
        xs = post_ln(xs, cross_attn(xs, cache_mem_k[l], cache_mem_v[l], cross_wq[l], cross_wo[l]), ln2_g[l], ln2_b[l])
        f_s, conv_s = conv_ffn(xs, state_conv[l], ffn_w_up[l], ffn_conv_w[l], ffn_conv_b[l], ffn_w_down[l])
        xs = post_ln(xs, f_s, ln3_g[l], ln3_b[l])
        ks_l.append(dk.reshape(Bs, Ts, DIFF_HEADS, 2 * DIFF_DH))
        vs_l.append(dv)
        gs_l.append(s_s)
        cs_l.append(conv_s)

    new_k_prompt = jnp.stack(kp_l)
    new_v_prompt = jnp.stack(vp_l)
    new_k_sample = jnp.stack(ks_l)
    new_v_sample = jnp.stack(vs_l)
    new_gla_prompt = jnp.stack(gp_l)
    new_gla_sample = jnp.stack(gs_l)
    new_conv_prompt = jnp.stack(cp_l)
    new_conv_sample = jnp.stack(cs_l)
    new_mem_k_prompt = jnp.stack(mkp_l)
    new_mem_v_prompt = jnp.stack(mvp_l)
    return (xp, xs, new_k_prompt, new_v_prompt, new_k_sample, new_v_sample, new_gla_prompt, new_gla_sample,
            new_conv_prompt, new_conv_sample, new_mem_k_prompt, new_mem_v_prompt)
```

```python
import functools
import math

import jax
import jax.numpy as jnp
from jax import lax
from jax.experimental import pallas as pl
from jax.experimental.pallas import tpu as pltpu

F32 = jnp.float32
BF16 = jnp.bfloat16

D_MODEL = 1024
PAGE_SIZE = 128
GLA_HEADS = 4
GLA_DK = 64
GLA_DV = 128
GLA_RANK = 16
GLA_TAU = 16.0
GLA_CHUNK = 64
GLA_SUB = 16
DIFF_HEADS = 4
DIFF_DH = 64
DIFF_DV = 128
ROPE_THETA = 10000.0
MEM_LEN = 256
MEM_HEADS = 4
MEM_DH = D_MODEL // MEM_HEADS
D_FF = 2816
CONV_W = 3
LN_EPS = 1e-5
LOG2E = 1.4426950408889634

GLA_QK = GLA_HEADS * GLA_DK
GLA_V = GLA_HEADS * GLA_DV
DIFF_QK = DIFF_HEADS * 2 * DIFF_DH
DIFF_V = DIFF_HEADS * DIFF_DV
LANES = 128
N_IN_PAD = 2 * GLA_QK + 2 * GLA_V + 2 * DIFF_QK + DIFF_V + LANES
SAMPLE_PAD = 16
VMEM_LIMIT = 52 * 1024 * 1024


def _cparams(sem):
    return pltpu.CompilerParams(dimension_semantics=sem, vmem_limit_bytes=VMEM_LIMIT)


def _post_ln(res, sub, g, b, alpha):
    h = alpha * res + sub
    mu = jnp.mean(h, axis=-1, keepdims=True)
    hc = h - mu
    var = jnp.mean(hc * hc, axis=-1, keepdims=True)
    return hc * lax.rsqrt(var + LN_EPS) * g + b


def _dot(a, b):
    return jnp.dot(a, b, preferred_element_type=F32)


def _dot_nt(a, b):
    return lax.dot_general(a, b, (((1,), (1,)), ((), ())), preferred_element_type=F32)


def _dot_tn(a, b):
    return lax.dot_general(a, b, (((0,), (0,)), ((), ())), preferred_element_type=F32)


_O_GQ, _O_GK, _O_GV, _O_GR = 0, GLA_QK, 2 * GLA_QK, 2 * GLA_QK + GLA_V
_O_DQ = 2 * GLA_QK + 2 * GLA_V
_O_DK = _O_DQ + DIFF_QK
_O_DV = _O_DK + DIFF_QK
_O_GA = _O_DV + DIFF_V


def _inproj_kernel(x_ref, w_ref, wa2_ref, ba_ref, c_ref, sa_ref, sb_ref,
                   gq_ref, gk_ref, gg_ref, gv_ref, gr_ref, dq_ref, dkf_ref, dvf_ref, dkb_ref, dvb_ref):
    h = _dot(x_ref[...].astype(BF16), w_ref[...])
    gq_ref[...] = h[:, _O_GQ:_O_GQ + GLA_QK] * (GLA_DK ** -0.5)
    gk_ref[...] = h[:, _O_GK:_O_GK + GLA_QK]
    gv_ref[...] = h[:, _O_GV:_O_GV + GLA_V].astype(BF16)
    gr_ref[...] = h[:, _O_GR:_O_GR + GLA_V]
    xg = _dot(h[:, _O_GA:_O_GA + LANES].astype(BF16), wa2_ref[...]) + ba_ref[...]
    gg_ref[...] = (jnp.minimum(xg, 0.0) - jnp.log1p(jnp.exp(-jnp.abs(xg)))) * (1.0 / GLA_TAU)
    c, sa, sb = c_ref[...], sa_ref[...], sb_ref[...]

    def rope(v):
        return v * c + pltpu.roll(v, LANES - DIFF_DH // 2, 1) * sa + pltpu.roll(v, DIFF_DH // 2, 1) * sb

    qscale = (DIFF_DH ** -0.5) * LOG2E
    for j in range(DIFF_QK // LANES):
        sl = slice(j * LANES, (j + 1) * LANES)
        dq_ref[:, sl] = (rope(h[:, _O_DQ + j * LANES:_O_DQ + (j + 1) * LANES]) * qscale).astype(BF16)
        rk = rope(h[:, _O_DK + j * LANES:_O_DK + (j + 1) * LANES])
        dkf_ref[:, sl] = rk
        dkb_ref[:, sl] = rk.astype(BF16)
    dv = h[:, _O_DV:_O_DV + DIFF_V]
    dvf_ref[...] = dv
    dvb_ref[...] = dv.astype(BF16)


def _inproj(x, w, wa2, ba, tabs, tab_blocks, tm):
    m = x.shape[0]
    row = lambda n: pl.BlockSpec((tm, n), lambda i: (i, 0))
    full = lambda a: pl.BlockSpec(a.shape, lambda i: (0,) * a.ndim)
    tab = pl.BlockSpec((tm, LANES), lambda i: (i % tab_blocks, 0))
    shapes = [(GLA_QK, F32), (GLA_QK, F32), (GLA_QK, F32), (GLA_V, BF16), (GLA_V, F32),
              (DIFF_QK, BF16), (DIFF_QK, F32), (DIFF_V, F32), (DIFF_QK, BF16), (DIFF_V, BF16)]
    return pl.pallas_call(
        _inproj_kernel,
        grid=(m // tm,),
        in_specs=[row(D_MODEL), full(w), full(wa2), full(ba), tab, tab, tab],
        out_specs=[row(n) for n, _ in shapes],
        out_shape=[jax.ShapeDtypeStruct((m, n), dt) for n, dt in shapes],
        compiler_params=_cparams(("parallel",)),
        name="inproj",
    )(x, w, wa2, ba, *tabs)


def _gla_kernel(*refs, tt, chunk, sub, t_valid, has_s0):
    if has_s0:
        q_ref, k_ref, g_ref, v_ref, gr_ref, nw_ref, s0_ref, o_ref, s_out_ref, st_scr, o_scr = refs
    else:
        q_ref, k_ref, g_ref, v_ref, gr_ref, nw_ref, o_ref, s_out_ref, st_scr, o_scr = refs
    i = pl.program_id(2)
    lane = lax.broadcasted_iota(jnp.int32, (1, LANES), 1)
    head_a = lane < GLA_DK

    st_mask = (lax.broadcasted_iota(jnp.int32, (2 * GLA_DV, LANES), 0) // GLA_DV
               == lax.broadcasted_iota(jnp.int32, (2 * GLA_DV, LANES), 1) // GLA_DK)

    @pl.when(i == 0)
    def _():
        if has_s0:
            s0t = s0_ref[...].reshape(2 * GLA_DK, GLA_DV).T
            st_scr[...] = jnp.where(st_mask, jnp.concatenate([s0t, s0t], axis=0), 0.0)
        else:
            st_scr[...] = jnp.zeros_like(st_scr)

    tril = (lax.broadcasted_iota(jnp.int32, (chunk, chunk), 0)
            >= lax.broadcasted_iota(jnp.int32, (chunk, chunk), 1)).astype(BF16)
    row_in_tile = lax.broadcasted_iota(jnp.int32, (chunk, 1), 0)

    for c in range(tt // chunk):
        rs = slice(c * chunk, (c + 1) * chunk)
        qc, kc, gc = q_ref[rs, :], k_ref[rs, :], g_ref[rs, :]
        vc = v_ref[rs, :]
        if t_valid is not None:
            ok = (row_in_tile + c * chunk) < t_valid
            kc = jnp.where(ok, kc, 0.0)
            gc = jnp.where(ok, gc, 0.0)
            vc = jnp.where(ok, vc, jnp.zeros_like(vc))
        g1 = gc.astype(BF16)
        r1 = gc - g1.astype(F32)
        g2 = r1.astype(BF16)
        g3 = (r1 - g2.astype(F32)).astype(BF16)
        b = _dot(tril, g1) + _dot(tril, g2) + _dot(tril, g3)
        b_last = b[chunk - 1:chunk, :]
        st = st_scr[...]
        qe = qc * jnp.exp(b)
        lhs = jnp.concatenate([jnp.where(head_a, qe, 0.0), jnp.where(head_a, 0.0, qe)], axis=0).astype(BF16)
        st_b = st.astype(BF16)
        oi_a = _dot_nt(lhs[0:chunk], st_b[0:GLA_DV])
        oi_b = _dot_nt(lhs[chunk:], st_b[GLA_DV:])
        o_inter = jnp.concatenate([oi_a, oi_b], axis=1)
        parts = []
        for s in range(chunk // sub):
            r0, r1_ = s * sub, (s + 1) * sub
            anchor = b[r0:r0 + 1, :]
            qs = qc[r0:r1_] * jnp.exp(b[r0:r1_] - anchor)
            ks = (kc[0:r1_] * jnp.exp(anchor - b[0:r1_])).astype(BF16)
            lq = jnp.concatenate([jnp.where(head_a, qs, 0.0), jnp.where(head_a, 0.0, qs)], axis=0).astype(BF16)
            att = _dot_nt(lq, ks)
            rr = lax.broadcasted_iota(jnp.int32, (2 * sub, r1_), 0)
            cc = lax.broadcasted_iota(jnp.int32, (2 * sub, r1_), 1)
            rr = jnp.where(rr >= sub, rr - sub, rr) + r0
            att = jnp.where(cc <= rr, att, 0.0).astype(BF16)
            ov = _dot(att, vc[0:r1_])
            parts.append(jnp.concatenate([ov[0:sub, 0:GLA_DV], ov[sub:, GLA_DV:]], axis=1))
        o_scr[rs, :] = o_inter + jnp.concatenate(parts, axis=0)
        kh = (kc * jnp.exp(b_last - b)).astype(BF16)
        upd = _dot_tn(vc, kh)
        st_scr[...] = st * jnp.exp(b_last) + jnp.where(st_mask, upd, 0.0)

    o = o_scr[...]
    nw = nw_ref[...]
    gr = gr_ref[...]
    outs = []
    for hh in range(2):
        oh = o[:, hh * GLA_DV:(hh + 1) * GLA_DV]
        ms = jnp.mean(oh * oh, axis=-1, keepdims=True)
        outs.append(oh * lax.rsqrt(ms + LN_EPS) * nw)
    gate = gr * (1.0 / (1.0 + jnp.exp(-gr)))
    o_ref[...] = (jnp.concatenate(outs, axis=1) * gate).astype(o_ref.dtype)

    @pl.when(i == pl.num_programs(2) - 1)
    def _():
        st = st_scr[...]
        sa = st[0:GLA_DV].T
        sb = st[GLA_DV:].T
        s_out_ref[...] = jnp.concatenate([sa[0:GLA_DK], sb[GLA_DK:]], axis=0).reshape(2, GLA_DK, GLA_DV)


def _gla(q, k, g, v, gr, nw, s0, tt, chunk, sub, t_valid):
    bsz, t, _ = q.shape
    has_s0 = s0 is not None
    qk_spec = pl.BlockSpec((None, tt, LANES), lambda b, p, i: (b, i, p))
    v_spec = pl.BlockSpec((None, tt, 2 * GLA_DV), lambda b, p, i: (b, i, p))
    s_spec = pl.BlockSpec((None, 2, GLA_DK, GLA_DV), lambda b, p, i: (b, p, 0, 0))
    in_specs = [qk_spec, qk_spec, qk_spec, v_spec, v_spec, pl.BlockSpec(nw.shape, lambda b, p, i: (0, 0))]
    args = [q, k, g, v, gr, nw]
    if has_s0:
        in_specs.append(s_spec)
        args.append(s0)
    return pl.pallas_call(
        functools.partial(_gla_kernel, tt=tt, chunk=chunk, sub=sub, t_valid=t_valid, has_s0=has_s0),
        grid=(bsz, GLA_HEADS // 2, t // tt),
        in_specs=in_specs,
        out_specs=[v_spec, s_spec],
        out_shape=[jax.ShapeDtypeStruct((bsz, t, GLA_V), BF16),
                   jax.ShapeDtypeStruct((bsz, GLA_HEADS, GLA_DK, GLA_DV), F32)],
        scratch_shapes=[pltpu.VMEM((2 * GLA_DV, LANES), F32), pltpu.VMEM((tt, 2 * GLA_DV), F32)],
        compiler_params=_cparams(("parallel", "parallel", "arbitrary")),
        name="gla",
    )(*args)


def _diff_lambda(lam_ref, lam_init):
    lf = lam_ref[...]
    a = jnp.sum(lf[0:1] * lf[1:2], axis=-1, keepdims=True)
    b = jnp.sum(lf[2:3] * lf[3:4], axis=-1, keepdims=True)
    return jnp.exp(a) - jnp.exp(b) + lam_init


def _diff_finish(o0, o1, lam, nw, lam_init):
    od = o0 - lam * o1
    ms = jnp.mean(od * od, axis=-1, keepdims=True)
    return od * lax.rsqrt(ms + LN_EPS) * nw * (1.0 - lam_init)


def _diff_prompt_kernel(lam_ref, nw_ref, q_ref, k_ref, v_ref, o_ref, m_scr, acc_scr, *, tq, lam_init):
    qi = pl.program_id(2)
    q = q_ref[...]
    lane = lax.broadcasted_iota(jnp.int32, (1, LANES), 1)
    zero = jnp.zeros_like(q)
    qz = jnp.concatenate([jnp.where(lane < DIFF_DH, q, zero), jnp.where(lane < DIFF_DH, zero, q)], axis=0)
    m_scr[...] = jnp.full_like(m_scr, -jnp.inf)
    acc_scr[...] = jnp.zeros_like(acc_scr)
    ones = jnp.ones((tq, LANES), BF16)

    def tile(ki, masked):
        ks = pl.multiple_of(ki * tq, tq)
        kt = k_ref[pl.ds(ks, tq), :]
        v1 = jnp.concatenate([v_ref[pl.ds(ks, tq), :], ones], axis=1)
        s = _dot_nt(qz, kt)
        if masked:
            rr = lax.broadcasted_iota(jnp.int32, (2 * tq, tq), 0)
            cc = lax.broadcasted_iota(jnp.int32, (2 * tq, tq), 1)
            rr = jnp.where(rr >= tq, rr - tq, rr)
            s = jnp.where(cc <= rr, s, -jnp.inf)
        m_prev = m_scr[...][:, 0:1]
        m_new = jnp.maximum(m_prev, jnp.max(s, axis=1, keepdims=True))
        alpha = jnp.exp2(m_prev - m_new)
        p = jnp.exp2(s - m_new).astype(BF16)
        acc_scr[...] = acc_scr[...] * alpha + _dot(p, v1)
        m_scr[...] = jnp.broadcast_to(m_new, m_scr.shape)

    def body(ki, carry):
        tile(ki, False)
        return carry

    lax.fori_loop(0, qi, body, 0)
    tile(qi, True)

    acc = acc_scr[...]
    o0 = acc[0:tq, 0:DIFF_DV] / acc[0:tq, DIFF_DV:DIFF_DV + 1]
    o1 = acc[tq:, 0:DIFF_DV] / acc[tq:, DIFF_DV:DIFF_DV + 1]
    lam = _diff_lambda(lam_ref, lam_init)
    o_ref[...] = _diff_finish(o0, o1, lam, nw_ref[...], lam_init).astype(o_ref.dtype)


def _diff_prompt(lam_p, nw, q, k, v, tq, lam_init):
    bsz, t, _ = q.shape
    qspec = pl.BlockSpec((None, tq, LANES), lambda b, h, i: (b, i, h))
    kvspec = pl.BlockSpec((None, t, LANES), lambda b, h, i: (b, 0, h))
    small = lambda a: pl.BlockSpec(a.shape, lambda b, h, i: (0,) * a.ndim)
    return pl.pallas_call(
        functools.partial(_diff_prompt_kernel, tq=tq, lam_init=lam_init),
        grid=(bsz, DIFF_HEADS, t // tq),
        in_specs=[small(lam_p), small(nw), qspec, kvspec, kvspec],
        out_specs=qspec,
        out_shape=jax.ShapeDtypeStruct((bsz, t, DIFF_V), BF16),
        scratch_shapes=[pltpu.VMEM((2 * tq, LANES), F32), pltpu.VMEM((2 * tq, 2 * LANES), F32)],
        compiler_params=_cparams(("parallel", "parallel", "arbitrary")),
        name="diff_prompt",
    )(lam_p, nw, q, k, v)


_QROWS = 8


def _diff_sample_kernel(pt_ref, lam_ref, nw_ref, q_ref, kn_ref, vn_ref, *rest, pages, t_new, lam_init):
    k_refs, v_refs = rest[:pages], rest[pages:2 * pages]
    o_ref, qz_scr, m_scr, l_scr, acc_scr = rest[2 * pages:]
    j = pl.program_id(1)

    def update(s, vmat):
        m_prev = m_scr[...][:, 0:1]
        m_new = jnp.maximum(m_prev, jnp.max(s, axis=1, keepdims=True))
        alpha = jnp.exp2(m_prev - m_new)
        p = jnp.exp2(s - m_new)
        l_scr[...] = jnp.broadcast_to(l_scr[...][:, 0:1] * alpha + jnp.sum(p, axis=1, keepdims=True), l_scr.shape)
        acc_scr[...] = acc_scr[...] * alpha + _dot(p.astype(BF16), vmat)
        m_scr[...] = jnp.broadcast_to(m_new, m_scr.shape)

    @pl.when(j == 0)
    def _():
        q = q_ref[...]
        lane = lax.broadcasted_iota(jnp.int32, (1, DIFF_QK), 1)
        blocks = []
        for c in range(2):
            for h in range(DIFF_HEADS):
                lo = h * 2 * DIFF_DH + c * DIFF_DH
                blocks.append(jnp.where((lane >= lo) & (lane < lo + DIFF_DH), q, 0.0))
        qz = jnp.concatenate(blocks, axis=0).astype(BF16)
        qz_scr[...] = qz
        m_scr[...] = jnp.full_like(m_scr, -jnp.inf)
        l_scr[...] = jnp.zeros_like(l_scr)
        acc_scr[...] = jnp.zeros_like(acc_scr)
        s = _dot_nt(qz, kn_ref[...])
        rr = lax.broadcasted_iota(jnp.int32, s.shape, 0) % _QROWS
        cc = lax.broadcasted_iota(jnp.int32, s.shape, 1)
        s = jnp.where((cc <= rr) & (cc < t_new), s, -jnp.inf)
        update(s, vn_ref[...])

    qz = qz_scr[...]
    kcat = jnp.concatenate([r[...].astype(BF16) for r in k_refs], axis=0)
    vcat = jnp.concatenate([r[...].astype(BF16) for r in v_refs], axis=0)
    update(_dot_nt(qz, kcat), vcat)

    @pl.when(j == pl.num_programs(1) - 1)
    def _():
        acc = acc_scr[...]
        inv = 1.0 / l_scr[...][:, 0:1]
        lam = _diff_lambda(lam_ref, lam_init)
        nw = nw_ref[...]
        outs = []
        for h in range(DIFF_HEADS):
            r0 = h * _QROWS
            r1 = (DIFF_HEADS + h) * _QROWS
            cs = slice(h * DIFF_DV, (h + 1) * DIFF_DV)
            o0 = acc[r0:r0 + _QROWS, cs] * inv[r0:r0 + _QROWS]
            o1 = acc[r1:r1 + _QROWS, cs] * inv[r1:r1 + _QROWS]
            outs.append(_diff_finish(o0, o1, lam, nw, lam_init))
        o_ref[...] = jnp.concatenate(outs, axis=1)


def _diff_sample(page_table, lam_p, nw, q, k_new, v_new, cache_k, cache_v, layer, pages, t_new, lam_init):
    bsz, n_pages = page_table.shape
    small = lambda a: pl.BlockSpec(a.shape, lambda b, j, pt: (0,) * a.ndim)
    qspec = pl.BlockSpec((None, _QROWS, DIFF_QK), lambda b, j, pt: (b, 0, 0))
    nspec = pl.BlockSpec((None, SAMPLE_PAD, DIFF_QK), lambda b, j, pt: (b, 0, 0))

    def page_spec(i):
        return pl.BlockSpec((None, None, PAGE_SIZE, DIFF_QK),
                            lambda b, j, pt: (layer, pt[b, j * pages + i], 0, 0))

    nrow = 2 * DIFF_HEADS * _QROWS
    grid_spec = pltpu.PrefetchScalarGridSpec(
        num_scalar_prefetch=1,
        grid=(bsz, n_pages // pages),
        in_specs=[small(lam_p), small(nw), qspec, nspec, nspec]
                 + [page_spec(i) for i in range(pages)] + [page_spec(i) for i in range(pages)],
        out_specs=qspec,
        scratch_shapes=[pltpu.VMEM((nrow, DIFF_QK), BF16), pltpu.VMEM((nrow, LANES), F32),
                        pltpu.VMEM((nrow, LANES), F32), pltpu.VMEM((nrow, DIFF_V), F32)],
    )
    return pl.pallas_call(
        functools.partial(_diff_sample_kernel, pages=pages, t_new=t_new, lam_init=lam_init),
        grid_spec=grid_spec,
        out_shape=jax.ShapeDtypeStruct((bsz, _QROWS, DIFF_V), F32),
        compiler_params=_cparams(("parallel", "arbitrary")),
        name="diff_sample",
    )(page_table, lam_p, nw, q, k_new, v_new, *([cache_k] * pages), *([cache_v] * pages))


def _memkv_kernel(x_ref, wk_ref, wv_ref, k_ref, v_ref):
    x = x_ref[...].astype(BF16)
    k_ref[...] = _dot(x, wk_ref[...])
    v_ref[...] = _dot(x, wv_ref[...])


def _memkv(mem, wk, wv):
    m = mem.shape[0]
    full = lambda a: pl.BlockSpec(a.shape, lambda i: (0, 0))
    return pl.pallas_call(
        _memkv_kernel,
        grid=(1,),
        in_specs=[full(mem), full(wk), full(wv)],
        out_specs=[full(mem), full(mem)],
        out_shape=[jax.ShapeDtypeStruct((m, D_MODEL), F32)] * 2,
        compiler_params=_cparams(("arbitrary",)),
        name="memkv",
    )(mem, wk, wv)


def _mix_cross_kernel(g_ref, d_ref, x_ref, wo1_ref, ln1g_ref, ln1b_ref, wq_ref, mk_ref, mv_ref, wo2_ref,
                      ln2g_ref, ln2b_ref, o_ref, *, alpha):
    mix = _dot(g_ref[...], wo1_ref[0:GLA_V, :]) + _dot(d_ref[...], wo1_ref[GLA_V:, :])
    x1 = _post_ln(x_ref[...], mix, ln1g_ref[...], ln1b_ref[...], alpha)
    q = (_dot(x1.astype(BF16), wq_ref[...]) * ((MEM_DH ** -0.5) * LOG2E)).astype(BF16)
    mk = mk_ref[...].astype(BF16)
    mv = mv_ref[...].astype(BF16)
    heads = []
    for h in range(MEM_HEADS):
        cs = slice(h * MEM_DH, (h + 1) * MEM_DH)
        s = _dot_nt(q[:, cs], mk[:, cs])
        p = jnp.exp2(s - jnp.max(s, axis=-1, keepdims=True))
        l = jnp.sum(p, axis=-1, keepdims=True)
        heads.append((_dot(p.astype(BF16), mv[:, cs]) / l).astype(BF16))
    y = _dot(jnp.concatenate(heads, axis=1), wo2_ref[...])
    o_ref[...] = _post_ln(x1, y, ln2g_ref[...], ln2b_ref[...], alpha)


def _mix_cross(g, d, x, w_out, ln1g, ln1b, wq, mk, mv, mem_layer, wo, ln2g, ln2b, tm, alpha):
    bsz, r, _ = x.shape
    row = lambda n: pl.BlockSpec((None, tm, n), lambda b, i: (b, i, 0))
    full = lambda a: pl.BlockSpec(a.shape, lambda b, i: (0,) * a.ndim)
    mem = pl.BlockSpec((None, None, MEM_LEN, D_MODEL), lambda b, i: (mem_layer, b, 0, 0))
    return pl.pallas_call(
        functools.partial(_mix_cross_kernel, alpha=alpha),
        grid=(bsz, r // tm),
        in_specs=[row(GLA_V), row(DIFF_V), row(D_MODEL), full(w_out), full(ln1g), full(ln1b), full(wq),
                  mem, mem, full(wo), full(ln2g), full(ln2b)],
        out_specs=row(D_MODEL),
        out_shape=jax.ShapeDtypeStruct(x.shape, F32),
        compiler_params=_cparams(("parallel", "parallel")),
        name="mix_cross",
    )(g, d, x, w_out, ln1g, ln1b, wq, mk, mv, wo, ln2g, ln2b)


_CARRY = 8


def _ffn_kernel(x_ref, st_ref, wua_ref, wub_ref, cw_ref, cb_ref, wd_ref, g_ref, b_ref,
                o_ref, ns_ref, carry_scr, acc_scr, *, tm, stride, n_valid, alpha):
    i = pl.program_id(1)
    f = pl.program_id(2)
    x = x_ref[...]
    xb = x.astype(BF16)
    rows = lax.broadcasted_iota(jnp.int32, (tm, 1), 0)

    if stride == 1:
        @pl.when(i == 0)
        def _():
            for half in range(2):
                carry_scr[f, half, _CARRY - 2:_CARRY - 1, :] = st_ref[half, 0]
                carry_scr[f, half, _CARRY - 1:_CARRY, :] = st_ref[half, 1]

    halves = []
    for half, w_ref in enumerate((wua_ref, wub_ref)):
        u = _dot(xb, w_ref[...])
        if stride == 1:
            p1 = carry_scr[f, half, _CARRY - 1:_CARRY, :]
            p2 = carry_scr[f, half, _CARRY - 2:_CARRY - 1, :]
            u1 = jnp.where(rows == 0, p1, pltpu.roll(u, 1, 0))
            u2 = jnp.where(rows == 0, p2, jnp.where(rows == 1, p1, pltpu.roll(u, 2, 0)))
            carry_scr[f, half] = u[tm - _CARRY:tm, :]
        else:
            padded = jnp.concatenate([st_ref[half, 0], st_ref[half, 1], u], axis=0)
            u1 = padded[stride:stride + tm]
            u2 = padded[0:tm]
        cw = cw_ref[half]
        halves.append(cb_ref[half] + cw[0:1] * u2 + cw[1:2] * u1 + cw[2:3] * u)
        for j in range(CONV_W - 1):
            r0 = (n_valid - (CONV_W - 1) + j) * stride
            ns_ref[half, j] = u[r0:r0 + stride, :]
    a, bb = halves
    hmid = (a * (1.0 / (1.0 + jnp.exp(-a))) * bb).astype(BF16)
    part = _dot(hmid, wd_ref[...])

    @pl.when(f == 0)
    def _():
        acc_scr[...] = part

    @pl.when(f > 0)
    def _():
        acc_scr[...] += part

    @pl.when(f == pl.num_programs(2) - 1)
    def _():
        o_ref[...] = _post_ln(x, acc_scr[...], g_ref[...], b_ref[...], alpha)


def _ffn(x, state, w_up, conv_w, conv_b, w_down, g, b, tm, tf, stride, n_valid, alpha):
    bsz, r, _ = x.shape
    nf = D_FF // tf
    assert stride == 1 or r == tm
    row = pl.BlockSpec((None, tm, D_MODEL), lambda bi, i, f: (bi, i, 0))
    st = pl.BlockSpec((None, 2, CONV_W - 1, stride, tf), lambda bi, i, f: (bi, 0, 0, 0, f))
    full = lambda a: pl.BlockSpec(a.shape, lambda bi, i, f: (0,) * a.ndim)
    return pl.pallas_call(
        functools.partial(_ffn_kernel, tm=tm, stride=stride, n_valid=n_valid, alpha=alpha),
        grid=(bsz, r // tm, nf),
        in_specs=[row, st,
                  pl.BlockSpec((D_MODEL, tf), lambda bi, i, f: (0, f)),
                  pl.BlockSpec((D_MODEL, tf), lambda bi, i, f: (0, nf + f)),
                  pl.BlockSpec((2, CONV_W, tf), lambda bi, i, f: (0, 0, f)),
                  pl.BlockSpec((2, 1, tf), lambda bi, i, f: (0, 0, f)),
                  pl.BlockSpec((tf, D_MODEL), lambda bi, i, f: (f, 0)),
                  full(g), full(b)],
        out_specs=[row, pl.BlockSpec((None, None, 2, CONV_W - 1, stride, tf), lambda bi, i, f: (bi, i, 0, 0, 0, f))],
        out_shape=[jax.ShapeDtypeStruct(x.shape, F32),
                   jax.ShapeDtypeStruct((bsz, r // tm, 2, CONV_W - 1, stride, D_FF), F32)],
        scratch_shapes=[pltpu.VMEM((nf, 2, _CARRY, tf), F32), pltpu.VMEM((tm, D_MODEL), F32)],
        compiler_params=_cparams(("parallel", "arbitrary", "arbitrary")),
        name="ffn",
    )(x, state, w_up, w_up, conv_w, conv_b, w_down, g, b)


def _rope_tables(pos):
    half = DIFF_DH // 2
    inv = ROPE_THETA ** (-jnp.arange(half, dtype=F32) / half)
    ang = pos.astype(F32)[:, None] * inv[None, :]
    cos = jnp.tile(jnp.cos(ang), (1, LANES // half))
    sin = jnp.tile(jnp.sin(ang), (1, LANES // half))
    first = (jnp.arange(LANES) % DIFF_DH) < half
    return cos, jnp.where(first, -sin, 0.0), jnp.where(first, 0.0, sin)


def _conv_state_in(s):
    return s.reshape(s.shape[0], CONV_W - 1, 2, D_FF).transpose(2, 1, 0, 3)[None]


def _conv_state_out(s):
    return s[0].transpose(2, 1, 0, 3).reshape(s.shape[3], CONV_W - 1, 2 * D_FF)


def kernel(x_prompt, x_sample, mem_prompt, cache_k, cache_v, page_table, cache_mem_k, cache_mem_v,
           state_gla, state_conv, w_in, gla_wa2, gla_ba, gla_norm_w, diff_lambda, diff_norm_w, w_out,
           ln1_g, ln1_b, cross_wq, cross_wk, cross_wv, cross_wo, ln2_g, ln2_b,
           ffn_w_up, ffn_conv_w, ffn_conv_b, ffn_w_down, ln3_g, ln3_b):
    depth = w_in.shape[0]
    bp, tp, _ = x_prompt.shape
    bs, ts, _ = x_sample.shape
    n_pages = page_table.shape[1]
    past_len = n_pages * PAGE_SIZE
    alpha = (2 * depth) ** 0.25
    assert ts <= _QROWS and tp % 512 == 0

    tm_a = 512
    tm_p = 512
    tq = 512
    gla_tt = 256
    ffn_tf = D_FF // 2
    pages = 8

    tabs_p = _rope_tables(jnp.arange(tp))
    pos_s = past_len + jnp.minimum(jnp.arange(SAMPLE_PAD), ts - 1)
    tabs_s = _rope_tables(jnp.tile(pos_s, bs))

    xp = x_prompt.reshape(bp * tp, D_MODEL)
    xs = jnp.pad(x_sample, ((0, 0), (0, SAMPLE_PAD - ts), (0, 0))).reshape(bs * SAMPLE_PAD, D_MODEL)
    mem2 = mem_prompt.reshape(bp * MEM_LEN, D_MODEL)
    zero_conv = jnp.zeros((bp, 2, CONV_W - 1, 1, D_FF), F32)
    n_phys = cache_k.shape[1]
    cache_k4 = cache_k.reshape(depth, n_phys, PAGE_SIZE, DIFF_QK)
    cache_v4 = cache_v.reshape(depth, n_phys, PAGE_SIZE, DIFF_V)
    cmem_k = cache_mem_k.reshape(depth, bs, MEM_LEN, D_MODEL)
    cmem_v = cache_mem_v.reshape(depth, bs, MEM_LEN, D_MODEL)

    outs = [[] for _ in range(10)]
    for l in range(depth):
        lam_init = 0.8 - 0.6 * math.exp(-0.3 * l)
        wl = w_in[l]
        ga0 = 2 * GLA_QK + 2 * GLA_V
        w_in_l = jnp.concatenate(
            [wl[:, :ga0], wl[:, ga0 + GLA_RANK:], wl[:, ga0:ga0 + GLA_RANK],
             jnp.zeros((D_MODEL, LANES - GLA_RANK), F32)], axis=1).astype(BF16)
        wa2_l = jnp.pad(gla_wa2[l], ((0, LANES - GLA_RANK), (0, 0))).astype(BF16)
        ba_l = gla_ba[l].reshape(1, GLA_QK)
        gnw = gla_norm_w[l].reshape(1, GLA_DV)
        dnw = diff_norm_w[l].reshape(1, DIFF_DV)
        w_out_l = w_out[l].astype(BF16)
        wq_l = cross_wq[l].astype(BF16)
        wk_l = cross_wk[l].astype(BF16)
        wv_l = cross_wv[l].astype(BF16)
        wo_l = cross_wo[l].astype(BF16)
        w_up_l = ffn_w_up[l].astype(BF16)
        w_down_l = ffn_w_down[l].astype(BF16)
        cw_l = ffn_conv_w[l].reshape(CONV_W, 2, D_FF).transpose(1, 0, 2)
        cb_l = ffn_conv_b[l].reshape(2, 1, D_FF)
        r1 = lambda a: a[l].reshape(1, D_MODEL)
        ln = [r1(a) for a in (ln1_g, ln1_b, ln2_g, ln2_b, ln3_g, ln3_b)]

        gq, gk, gg, gv, gr, dq, dkf, dvf, dkb, dvb = _inproj(xp, w_in_l, wa2_l, ba_l, tabs_p, tp // tm_a, tm_a)
        b3 = lambda a: a.reshape(bp, tp, a.shape[-1])
        go, s_p = _gla(b3(gq), b3(gk), b3(gg), b3(gv), b3(gr), gnw, None, gla_tt, GLA_CHUNK, GLA_SUB, None)
        do = _diff_prompt(diff_lambda[l], dnw, b3(dq), b3(dkb), b3(dvb), tq, lam_init)
        mk_p, mv_p = _memkv(mem2, wk_l, wv_l)
        x2 = _mix_cross(go, do, xp.reshape(bp, tp, D_MODEL), w_out_l, ln[0], ln[1], wq_l,
                        mk_p.reshape(1, bp, MEM_LEN, D_MODEL), mv_p.reshape(1, bp, MEM_LEN, D_MODEL), 0,
                        wo_l, ln[2], ln[3], tm_p, alpha)
        x3, conv_p = _ffn(x2, zero_conv, w_up_l, cw_l, cb_l, w_down_l, ln[4], ln[5], tm_p, ffn_tf, 1, tm_p, alpha)
        xp = x3.reshape(bp * tp, D_MODEL)
        outs[0].append(dkf.reshape(bp, tp // PAGE_SIZE, PAGE_SIZE, DIFF_HEADS, 2 * DIFF_DH))
        outs[1].append(dvf.reshape(bp, tp // PAGE_SIZE, PAGE_SIZE, DIFF_HEADS, DIFF_DV))
        outs[4].append(s_p)
        outs[6].append(conv_p[:, -1].transpose(0, 2, 1, 3, 4).reshape(bp, CONV_W - 1, 2 * D_FF))
        outs[8].append(mk_p.reshape(bp, MEM_LEN, MEM_HEADS, MEM_DH))
        outs[9].append(mv_p.reshape(bp, MEM_LEN, MEM_HEADS, MEM_DH))

        gq, gk, gg, gv, gr, dq, dkf, dvf, dkb, dvb = _inproj(xs, w_in_l, wa2_l, ba_l, tabs_s, 1, bs * SAMPLE_PAD)
        s3 = lambda a: a.reshape(bs, SAMPLE_PAD, a.shape[-1])
        go, s_s = _gla(s3(gq), s3(gk), s3(gg), s3(gv), s3(gr), gnw, state_gla[l],
                       SAMPLE_PAD, SAMPLE_PAD, SAMPLE_PAD, ts)
        dq_s = s3(dq)[:, :_QROWS].astype(F32)
        do8 = _diff_sample(page_table, diff_lambda[l], dnw, dq_s, s3(dkb), s3(dvb),
                           cache_k4, cache_v4, l, pages, ts, lam_init)
        do = jnp.pad(do8, ((0, 0), (0, SAMPLE_PAD - _QROWS), (0, 0))).astype(BF16)
        x2 = _mix_cross(go, do, xs.reshape(bs, SAMPLE_PAD, D_MODEL), w_out_l, ln[0], ln[1], wq_l,
                        cmem_k, cmem_v, l, wo_l, ln[2], ln[3], SAMPLE_PAD, alpha)
        x2t = x2.transpose(1, 0, 2).reshape(1, SAMPLE_PAD * bs, D_MODEL)
        x3t, conv_s = _ffn(x2t, _conv_state_in(state_conv[l]), w_up_l, cw_l, cb_l, w_down_l, ln[4], ln[5],
                           SAMPLE_PAD * bs, ffn_tf, bs, ts, alpha)
        xs = x3t.reshape(SAMPLE_PAD, bs, D_MODEL).transpose(1, 0, 2).reshape(bs * SAMPLE_PAD, D_MODEL)
        outs[2].append(s3(dkf)[:, :ts].reshape(bs, ts, DIFF_HEADS, 2 * DIFF_DH))
        outs[3].append(s3(dvf)[:, :ts].reshape(bs, ts, DIFF_HEADS, DIFF_DV))
        outs[5].append(s_s)
        outs[7].append(_conv_state_out(conv_s[:, -1]))

    st = [jnp.stack(o) for o in outs]
    y_prompt = xp.reshape(bp, tp, D_MODEL)
    y_sample = xs.reshape(bs, SAMPLE_PAD, D_MODEL)[:, :ts]
    return (y_prompt, y_sample, st[0], st[1], st[2], st[3], st[4], st[5], st[6], st[7], st[8], st[9])
```

```python
import functools
import math

import jax
import jax.numpy as jnp
from jax import lax
from jax.experimental import pallas as pl
from jax.experimental.pallas import tpu as pltpu

F32 = jnp.float32
BF16 = jnp.bfloat16

D_MODEL = 1024
PAGE_SIZE = 128
GLA_HEADS = 4
GLA_DK = 64
GLA_DV = 128
GLA_RANK = 16
GLA_TAU = 16.0
GLA_CHUNK = 64
GLA_SUB = 16
DIFF_HEADS = 4
DIFF_DH = 64
DIFF_DV = 128
ROPE_THETA = 10000.0
MEM_LEN = 256
MEM_HEADS = 4
MEM_DH = D_MODEL // MEM_HEADS
D_FF = 2816
CONV_W = 3
LN_EPS = 1e-5
LOG2E = 1.4426950408889634

GLA_QK = GLA_HEADS * GLA_DK
GLA_V = GLA_HEADS * GLA_DV
DIFF_QK = DIFF_HEADS * 2 * DIFF_DH
DIFF_V = DIFF_HEADS * DIFF_DV
LANES = 128
N_IN_PAD = 2 * GLA_QK + 2 * GLA_V + 2 * DIFF_QK + DIFF_V + LANES
SAMPLE_PAD = 16
VMEM_LIMIT = 52 * 1024 * 1024


def _cparams(sem):
    return pltpu.CompilerParams(dimension_semantics=sem, vmem_limit_bytes=VMEM_LIMIT)


def _post_ln(res, sub, g, b, alpha):
    h = alpha * res + sub
    mu = jnp.mean(h, axis=-1, keepdims=True)
    hc = h - mu
    var = jnp.mean(hc * hc, axis=-1, keepdims=True)
    return hc * lax.rsqrt(var + LN_EPS) * g + b


def _dot(a, b):
    return jnp.dot(a, b, preferred_element_type=F32)


def _dot_nt(a, b):
    return lax.dot_general(a, b, (((1,), (1,)), ((), ())), preferred_element_type=F32)


def _dot_tn(a, b):
    return lax.dot_general(a, b, (((0,), (0,)), ((), ())), preferred_element_type=F32)


_O_GQ, _O_GK, _O_GV, _O_GR = 0, GLA_QK, 2 * GLA_QK, 2 * GLA_QK + GLA_V
_O_DQ = 2 * GLA_QK + 2 * GLA_V
_O_DK = _O_DQ + DIFF_QK
_O_DV = _O_DK + DIFF_QK
_O_GA = _O_DV + DIFF_V


def _inproj_kernel(x_ref, w_ref, wa2_ref, ba_ref, c_ref, sa_ref, sb_ref,
                   gq_ref, gk_ref, gg_ref, gv_ref, gr_ref, dq_ref, dkf_ref, dvf_ref, dkb_ref, dvb_ref):
    h = _dot(x_ref[...].astype(BF16), w_ref[...])
    gq_ref[...] = h[:, _O_GQ:_O_GQ + GLA_QK] * (GLA_DK ** -0.5)
    gk_ref[...] = h[:, _O_GK:_O_GK + GLA_QK]
    gv_ref[...] = h[:, _O_GV:_O_GV + GLA_V].astype(BF16)
    gr_ref[...] = h[:, _O_GR:_O_GR + GLA_V]
    xg = _dot(h[:, _O_GA:_O_GA + LANES].astype(BF16), wa2_ref[...]) + ba_ref[...]
    gg_ref[...] = (jnp.minimum(xg, 0.0) - jnp.log1p(jnp.exp(-jnp.abs(xg)))) * (1.0 / GLA_TAU)
    c, sa, sb = c_ref[...], sa_ref[...], sb_ref[...]

    def rope(v):
        return v * c + pltpu.roll(v, LANES - DIFF_DH // 2, 1) * sa + pltpu.roll(v, DIFF_DH // 2, 1) * sb

    qscale = (DIFF_DH ** -0.5) * LOG2E
    tm = h.shape[0]
    for j in range(DIFF_HEADS):
        sl = slice(j * LANES, (j + 1) * LANES)
        head_rows = pl.ds(j, tm, stride=DIFF_HEADS)
        dq_ref[:, sl] = (rope(h[:, _O_DQ + j * LANES:_O_DQ + (j + 1) * LANES]) * qscale).astype(BF16)
        rk = rope(h[:, _O_DK + j * LANES:_O_DK + (j + 1) * LANES])
        dkf_ref[head_rows, :] = rk
        dkb_ref[:, sl] = rk.astype(BF16)
        dv = h[:, _O_DV + j * LANES:_O_DV + (j + 1) * LANES]
        dvf_ref[head_rows, :] = dv
        dvb_ref[:, sl] = dv.astype(BF16)


def _inproj(x, w, wa2, ba, tabs, tab_blocks, tm):
    m = x.shape[0]
    row = lambda n: pl.BlockSpec((tm, n), lambda i: (i, 0))
    full = lambda a: pl.BlockSpec(a.shape, lambda i: (0,) * a.ndim)
    tab = pl.BlockSpec((tm, LANES), lambda i: (i % tab_blocks, 0))
    shapes = [(1, GLA_QK, F32), (1, GLA_QK, F32), (1, GLA_QK, F32), (1, GLA_V, BF16), (1, GLA_V, F32),
              (1, DIFF_QK, BF16), (DIFF_HEADS, LANES, F32), (DIFF_HEADS, LANES, F32),
              (1, DIFF_QK, BF16), (1, DIFF_V, BF16)]
    return pl.pallas_call(
        _inproj_kernel,
        grid=(m // tm,),
        in_specs=[row(D_MODEL), full(w), full(wa2), full(ba), tab, tab, tab],
        out_specs=[pl.BlockSpec((tm * k, n), lambda i: (i, 0)) for k, n, _ in shapes],
        out_shape=[jax.ShapeDtypeStruct((m * k, n), dt) for k, n, dt in shapes],
        compiler_params=_cparams(("parallel",)),
        name="inproj",
    )(x, w, wa2, ba, *tabs)


def _gla_chunk(qc, kc, gc, vc, st, tril, head_a, st_mask, chunk, sub):
    nsub = chunk // sub
    g1 = gc.astype(BF16)
    r1 = gc - g1.astype(F32)
    g2 = r1.astype(BF16)
    g3 = (r1 - g2.astype(F32)).astype(BF16)
    b3 = _dot(tril, jnp.concatenate([g1, g2, g3], axis=1))
    b = b3[:, 0:LANES] + b3[:, LANES:2 * LANES] + b3[:, 2 * LANES:]
    b_last = b[chunk - 1:chunk, :]

    def by_head(x):
        return jnp.concatenate([jnp.where(head_a, x, 0.0), jnp.where(head_a, 0.0, x)], axis=0).astype(BF16)

    oi = _dot_nt(by_head(qc * jnp.exp(b)), st.astype(BF16))
    o_inter = jnp.concatenate([oi[0:chunk, 0:GLA_DV], oi[chunk:, GLA_DV:]], axis=1)
    key_row = lax.broadcasted_iota(jnp.int32, (chunk, 1), 0)
    lqs, kss = [], []
    for s in range(nsub):
        r0, r1_ = s * sub, (s + 1) * sub
        anchor = b[r0:r0 + 1, :]
        lqs.append(by_head(qc[r0:r1_] * jnp.exp(b[r0:r1_] - anchor)))
        kss.append((kc * jnp.exp(jnp.where(key_row < r1_, anchor - b, 0.0))).astype(BF16))
    att = lax.dot_general(jnp.stack(lqs), jnp.stack(kss), (((2,), (2,)), ((0,), (0,))),
                          preferred_element_type=F32)
    q_tok = (lax.broadcasted_iota(jnp.int32, att.shape, 0) * sub
             + lax.broadcasted_iota(jnp.int32, att.shape, 1) % sub)
    att = jnp.where(lax.broadcasted_iota(jnp.int32, att.shape, 2) <= q_tok, att, 0.0)
    ov = _dot(att.reshape(nsub * 2 * sub, chunk).astype(BF16), vc)
    o_intra = jnp.concatenate(
        [jnp.concatenate([ov[2 * s * sub:(2 * s + 1) * sub, 0:GLA_DV],
                          ov[(2 * s + 1) * sub:(2 * s + 2) * sub, GLA_DV:]], axis=1) for s in range(nsub)], axis=0)
    upd = _dot_tn(vc, (kc * jnp.exp(b_last - b)).astype(BF16))
    return o_inter + o_intra, st * jnp.exp(b_last) + jnp.where(st_mask, upd, 0.0)


def _gla_kernel(*refs, bb, tt, chunk, sub, t_valid, has_s0):
    if has_s0:
        q_ref, k_ref, g_ref, v_ref, gr_ref, nw_ref, s0_ref, o_ref, s_out_ref, st_scr = refs
    else:
        q_ref, k_ref, g_ref, v_ref, gr_ref, nw_ref, o_ref, s_out_ref, st_scr = refs
    i = pl.program_id(1)
    pairs = GLA_HEADS // 2
    head_a = lax.broadcasted_iota(jnp.int32, (1, LANES), 1) < GLA_DK
    st_mask = (lax.broadcasted_iota(jnp.int32, (2 * GLA_DV, LANES), 0) // GLA_DV
               == lax.broadcasted_iota(jnp.int32, (2 * GLA_DV, LANES), 1) // GLA_DK)
    tril = (lax.broadcasted_iota(jnp.int32, (chunk, chunk), 0)
            >= lax.broadcasted_iota(jnp.int32, (chunk, chunk), 1)).astype(BF16)
    row_in_chunk = lax.broadcasted_iota(jnp.int32, (chunk, 1), 0)

    @pl.when(i == 0)
    def _():
        for bi in range(bb):
            for p in range(pairs):
                if has_s0:
                    s0t = s0_ref[bi, 2 * p:2 * p + 2].reshape(2 * GLA_DK, GLA_DV).T
                    st_scr[bi, p] = jnp.where(st_mask, jnp.concatenate([s0t, s0t], axis=0), 0.0)
                else:
                    st_scr[bi, p] = jnp.zeros((2 * GLA_DV, LANES), F32)

    nw = nw_ref[...]
    for bi in range(bb):
        for p in range(pairs):
            ks = slice(p * LANES, (p + 1) * LANES)
            vs = slice(p * 2 * GLA_DV, (p + 1) * 2 * GLA_DV)
            st = st_scr[bi, p]
            for c in range(tt // chunk):
                rs = slice(c * chunk, (c + 1) * chunk)
                qc, kc, gc, vc = q_ref[bi, rs, ks], k_ref[bi, rs, ks], g_ref[bi, rs, ks], v_ref[bi, rs, vs]
                if t_valid is not None:
                    ok = (row_in_chunk + c * chunk) < t_valid
                    kc = jnp.where(ok, kc, 0.0)
                    gc = jnp.where(ok, gc, 0.0)
                    vc = jnp.where(ok, vc, jnp.zeros_like(vc))
                o, st = _gla_chunk(qc, kc, gc, vc, st, tril, head_a, st_mask, chunk, sub)
                gr = gr_ref[bi, rs, vs]
                normed = []
                for hh in range(2):
                    oh = o[:, hh * GLA_DV:(hh + 1) * GLA_DV]
                    normed.append(oh * lax.rsqrt(jnp.mean(oh * oh, axis=-1, keepdims=True) + LN_EPS) * nw)
                gate = gr * (1.0 / (1.0 + jnp.exp(-gr)))
                o_ref[bi, rs, vs] = (jnp.concatenate(normed, axis=1) * gate).astype(o_ref.dtype)
            st_scr[bi, p] = st

    @pl.when(i == pl.num_programs(1) - 1)
    def _():
        for bi in range(bb):
            for p in range(pairs):
                st = st_scr[bi, p]
                sa = st[0:GLA_DV].T
                sb = st[GLA_DV:].T
                s_out_ref[bi, 2 * p:2 * p + 2] = jnp.concatenate(
                    [sa[0:GLA_DK], sb[GLA_DK:]], axis=0).reshape(2, GLA_DK, GLA_DV)


def _gla(q, k, g, v, gr, nw, s0, bb, tt, chunk, sub, t_valid):
    bsz, t, _ = q.shape
    has_s0 = s0 is not None
    qk_spec = pl.BlockSpec((bb, tt, GLA_QK), lambda b, i: (b, i, 0))
    v_spec = pl.BlockSpec((bb, tt, GLA_V), lambda b, i: (b, i, 0))
    s_spec = pl.BlockSpec((bb, GLA_HEADS, GLA_DK, GLA_DV), lambda b, i: (b, 0, 0, 0))
    in_specs = [qk_spec, qk_spec, qk_spec, v_spec, v_spec, pl.BlockSpec(nw.shape, lambda b, i: (0, 0))]
    args = [q, k, g, v, gr, nw]
    if has_s0:
        in_specs.append(s_spec)
        args.append(s0)
    return pl.pallas_call(
        functools.partial(_gla_kernel, bb=bb, tt=tt, chunk=chunk, sub=sub, t_valid=t_valid, has_s0=has_s0),
        grid=(bsz // bb, t // tt),
        in_specs=in_specs,
        out_specs=[v_spec, s_spec],
        out_shape=[jax.ShapeDtypeStruct((bsz, t, GLA_V), BF16),
                   jax.ShapeDtypeStruct((bsz, GLA_HEADS, GLA_DK, GLA_DV), F32)],
        scratch_shapes=[pltpu.VMEM((bb, GLA_HEADS // 2, 2 * GLA_DV, LANES), F32)],
        compiler_params=_cparams(("parallel", "arbitrary")),
        name="gla",
    )(*args)


def _diff_lambda(lam_ref, lam_init):
    lf = lam_ref[...]
    a = jnp.sum(lf[0:1] * lf[1:2], axis=-1, keepdims=True)
    b = jnp.sum(lf[2:3] * lf[3:4], axis=-1, keepdims=True)
    return jnp.exp(a) - jnp.exp(b) + lam_init


def _diff_finish(o0, o1, lam, nw, lam_init):
    od = o0 - lam * o1
    ms = jnp.mean(od * od, axis=-1, keepdims=True)
    return od * lax.rsqrt(ms + LN_EPS) * nw * (1.0 - lam_init)


def _diff_prompt_kernel(lam_ref, nw_ref, q_ref, k_ref, v_ref, o_ref,
                        qz_scr, s_scr, p_scr, m_scr, a_scr, acc_scr, *, tq, rc, lam_init):
    qi = pl.program_id(2)
    q = q_ref[...]
    lane = lax.broadcasted_iota(jnp.int32, (1, LANES), 1)
    zero = jnp.zeros_like(q)
    qz_scr[0:tq, :] = jnp.where(lane < DIFF_DH, q, zero)
    qz_scr[tq:, :] = jnp.where(lane < DIFF_DH, zero, q)
    m_scr[...] = jnp.full_like(m_scr, -jnp.inf)
    acc_scr[...] = jnp.zeros_like(acc_scr)
    p_scr[1] = jnp.zeros(p_scr.shape[1:], BF16)
    a_scr[1] = jnp.ones(a_scr.shape[1:], F32)
    ones = jnp.ones((tq, LANES), BF16)

    def scores(t, slot):
        ks = pl.multiple_of(t * tq, tq)
        s_scr[slot] = _dot_nt(qz_scr[...], k_ref[pl.ds(ks, tq), :])

    def softmax(slot, masked):
        for c in range(2 * tq // rc):
            rows = slice(c * rc, (c + 1) * rc)
            s = s_scr[slot, rows, :]
            if masked:
                rr = lax.broadcasted_iota(jnp.int32, (rc, 1), 0) + (c * rc) % tq
                cc = lax.broadcasted_iota(jnp.int32, (1, tq), 1)
                s = jnp.where(cc <= rr, s, -jnp.inf)
            m_prev = m_scr[rows, :]
            m_new = jnp.maximum(m_prev, jnp.max(s, axis=1, keepdims=True))
            a_scr[slot, rows, :] = jnp.exp2(m_prev - m_new)
            m_scr[rows, :] = m_new
            p_scr[slot, rows, :] = jnp.exp2(s - jnp.tile(m_new, (1, tq // LANES))).astype(BF16)

    def values(t, slot):
        ks = pl.multiple_of(t * tq, tq)
        v1 = jnp.concatenate([v_ref[pl.ds(ks, tq), :], ones], axis=1)
        a = a_scr[slot]
        acc_scr[...] = acc_scr[...] * jnp.concatenate([a, a], axis=1) + _dot(p_scr[slot], v1)

    scores(0, 0)

    def step(t, slot):
        scores(t + 1, 1 - slot)
        softmax(slot, False)
        values(jnp.maximum(t - 1, 0), 1 - slot)

    def body(u, carry):
        step(2 * u, 0)
        step(2 * u + 1, 1)
        return carry

    lax.fori_loop(0, qi // 2, body, 0)

    def tail(slot):
        softmax(slot, True)
        values(jnp.maximum(qi - 1, 0), 1 - slot)
        values(qi, slot)

    @pl.when(qi % 2 == 0)
    def _():
        tail(0)

    @pl.when(qi % 2 == 1)
    def _():
        softmax(0, False)
        values(jnp.maximum(qi - 2, 0), 1)
        values(qi - 1, 0)
        scores(qi, 1)
        softmax(1, True)
        values(qi, 1)

    acc = acc_scr[...]
    o0 = acc[0:tq, 0:DIFF_DV] / acc[0:tq, DIFF_DV:DIFF_DV + 1]
    o1 = acc[tq:, 0:DIFF_DV] / acc[tq:, DIFF_DV:DIFF_DV + 1]
    lam = _diff_lambda(lam_ref, lam_init)
    o_ref[...] = _diff_finish(o0, o1, lam, nw_ref[...], lam_init).astype(o_ref.dtype)


def _diff_prompt(lam_p, nw, q, k, v, tq, lam_init):
    bsz, t, _ = q.shape
    qspec = pl.BlockSpec((None, tq, LANES), lambda b, h, i: (b, i, h))
    kvspec = pl.BlockSpec((None, t, LANES), lambda b, h, i: (b, 0, h))
    small = lambda a: pl.BlockSpec(a.shape, lambda b, h, i: (0,) * a.ndim)
    return pl.pallas_call(
        functools.partial(_diff_prompt_kernel, tq=tq, rc=32, lam_init=lam_init),
        grid=(bsz, DIFF_HEADS, t // tq),
        in_specs=[small(lam_p), small(nw), qspec, kvspec, kvspec],
        out_specs=qspec,
        out_shape=jax.ShapeDtypeStruct((bsz, t, DIFF_V), BF16),
        scratch_shapes=[pltpu.VMEM((2 * tq, LANES), BF16),
                        pltpu.VMEM((2, 2 * tq, tq), F32),
                        pltpu.VMEM((2, 2 * tq, tq), BF16),
                        pltpu.VMEM((2 * tq, LANES), F32),
                        pltpu.VMEM((2, 2 * tq, LANES), F32),
                        pltpu.VMEM((2 * tq, 2 * LANES), F32)],
        compiler_params=_cparams(("parallel", "parallel", "arbitrary")),
        name="diff_prompt",
    )(lam_p, nw, q, k, v)


_QROWS = 8


def _diff_sample_kernel(pt_ref, lam_ref, nw_ref, q_ref, kn_ref, vn_ref, *rest, pages, t_new, lam_init):
    k_refs, v_refs = rest[:pages], rest[pages:2 * pages]
    o_ref, qz_scr, m_scr, l_scr, acc_scr = rest[2 * pages:]
    j = pl.program_id(1)
    grp = 2 * _QROWS

    def update(h, s, vmat):
        rows = slice(h * grp, (h + 1) * grp)
        m_prev = m_scr[rows, 0:1]
        m_new = jnp.maximum(m_prev, jnp.max(s, axis=1, keepdims=True))
        alpha = jnp.exp2(m_prev - m_new)
        p = jnp.exp2(s - m_new)
        l_new = l_scr[rows, 0:1] * alpha + jnp.sum(p, axis=1, keepdims=True)
        l_scr[rows, :] = jnp.broadcast_to(l_new, (grp, LANES))
        acc_scr[rows, :] = acc_scr[rows, :] * alpha + _dot(p.astype(BF16), vmat)
        m_scr[rows, :] = jnp.broadcast_to(m_new, (grp, LANES))

    @pl.when(j == 0)
    def _():
        q = q_ref[...]
        lane = lax.broadcasted_iota(jnp.int32, (1, LANES), 1)
        m_scr[...] = jnp.full_like(m_scr, -jnp.inf)
        l_scr[...] = jnp.zeros_like(l_scr)
        acc_scr[...] = jnp.zeros_like(acc_scr)
        for h in range(DIFF_HEADS):
            qh = q[:, h * LANES:(h + 1) * LANES]
            qz = jnp.concatenate([jnp.where(lane < DIFF_DH, qh, 0.0), jnp.where(lane < DIFF_DH, 0.0, qh)],
                                 axis=0).astype(BF16)
            qz_scr[h * grp:(h + 1) * grp, :] = qz
            s = _dot_nt(qz, kn_ref[:, h * LANES:(h + 1) * LANES])
            rr = lax.broadcasted_iota(jnp.int32, s.shape, 0) % _QROWS
            cc = lax.broadcasted_iota(jnp.int32, s.shape, 1)
            s = jnp.where((cc <= rr) & (cc < t_new), s, -jnp.inf)
            update(h, s, vn_ref[:, h * LANES:(h + 1) * LANES])

    for h in range(DIFF_HEADS):
        head_rows = pl.ds(h, PAGE_SIZE, stride=DIFF_HEADS)
        kh = jnp.concatenate([r[head_rows, :].astype(BF16) for r in k_refs], axis=0)
        vh = jnp.concatenate([r[head_rows, :].astype(BF16) for r in v_refs], axis=0)
        update(h, _dot_nt(qz_scr[h * grp:(h + 1) * grp, :], kh), vh)

    @pl.when(j == pl.num_programs(1) - 1)
    def _():
        acc = acc_scr[...]
        inv = 1.0 / l_scr[:, 0:1]
        lam = _diff_lambda(lam_ref, lam_init)
        nw = nw_ref[...]
        outs = []
        for h in range(DIFF_HEADS):
            r0 = h * grp
            r1 = r0 + _QROWS
            o0 = acc[r0:r0 + _QROWS] * inv[r0:r0 + _QROWS]
            o1 = acc[r1:r1 + _QROWS] * inv[r1:r1 + _QROWS]
            outs.append(_diff_finish(o0, o1, lam, nw, lam_init))
        o_ref[...] = jnp.concatenate(outs, axis=1)


def _diff_sample(page_table, lam_p, nw, q, k_new, v_new, cache_k, cache_v, layer, pages, t_new, lam_init):
    bsz, n_pages = page_table.shape
    small = lambda a: pl.BlockSpec(a.shape, lambda b, j, pt: (0,) * a.ndim)
    qspec = pl.BlockSpec((None, _QROWS, DIFF_QK), lambda b, j, pt: (b, 0, 0))
    nspec = pl.BlockSpec((None, SAMPLE_PAD, DIFF_QK), lambda b, j, pt: (b, 0, 0))

    def page_spec(i):
        return pl.BlockSpec((None, None, PAGE_SIZE * DIFF_HEADS, LANES),
                            lambda b, j, pt: (layer, pt[b, j * pages + i], 0, 0))

    nrow = 2 * DIFF_HEADS * _QROWS
    grid_spec = pltpu.PrefetchScalarGridSpec(
        num_scalar_prefetch=1,
        grid=(bsz, n_pages // pages),
        in_specs=[small(lam_p), small(nw), qspec, nspec, nspec]
                 + [page_spec(i) for i in range(pages)] + [page_spec(i) for i in range(pages)],
        out_specs=qspec,
        scratch_shapes=[pltpu.VMEM((nrow, LANES), BF16), pltpu.VMEM((nrow, LANES), F32),
                        pltpu.VMEM((nrow, LANES), F32), pltpu.VMEM((nrow, DIFF_DV), F32)],
    )
    return pl.pallas_call(
        functools.partial(_diff_sample_kernel, pages=pages, t_new=t_new, lam_init=lam_init),
        grid_spec=grid_spec,
        out_shape=jax.ShapeDtypeStruct((bsz, _QROWS, DIFF_V), F32),
        compiler_params=_cparams(("parallel", "arbitrary")),
        name="diff_sample",
    )(page_table, lam_p, nw, q, k_new, v_new, *([cache_k] * pages), *([cache_v] * pages))


_MEM_SPLIT = D_MODEL // LANES
_MEM_PIECES = MEM_DH // LANES


def _mem_row(head, piece):
    return piece * MEM_HEADS + head


def _mem_rows_view(a):
    lead = a.shape[:-3]
    n = len(lead)
    a = a.reshape(*lead, MEM_LEN, MEM_HEADS, _MEM_PIECES, LANES)
    a = a.transpose(*range(n), n, n + 2, n + 1, n + 3)
    return a.reshape(*lead, MEM_LEN * _MEM_SPLIT, LANES)


def _mem_rows_unview(a):
    lead = a.shape[:-2]
    n = len(lead)
    a = a.reshape(*lead, MEM_LEN, _MEM_PIECES, MEM_HEADS, LANES)
    a = a.transpose(*range(n), n, n + 2, n + 1, n + 3)
    return a.reshape(*lead, MEM_LEN, MEM_HEADS, MEM_DH)


def _memkv_kernel(x_ref, wk_ref, wv_ref, k_ref, v_ref):
    x = x_ref[...].astype(BF16)
    m = x.shape[0]
    for w_ref, o_ref in ((wk_ref, k_ref), (wv_ref, v_ref)):
        y = _dot(x, w_ref[...])
        for j in range(_MEM_SPLIT):
            o_ref[pl.ds(_mem_row(j // _MEM_PIECES, j % _MEM_PIECES), m, stride=_MEM_SPLIT), :] = (
                y[:, j * LANES:(j + 1) * LANES])


def _memkv(mem, wk, wv):
    m = mem.shape[0]
    full = lambda a: pl.BlockSpec(a.shape, lambda i: (0, 0))
    out = pl.BlockSpec((m * _MEM_SPLIT, LANES), lambda i: (0, 0))
    return pl.pallas_call(
        _memkv_kernel,
        grid=(1,),
        in_specs=[full(mem), full(wk), full(wv)],
        out_specs=[out, out],
        out_shape=[jax.ShapeDtypeStruct((m * _MEM_SPLIT, LANES), F32)] * 2,
        compiler_params=_cparams(("arbitrary",)),
        name="memkv",
    )(mem, wk, wv)


def _mix_cross_kernel(g_ref, d_ref, x_ref, wo1_ref, ln1g_ref, ln1b_ref, wq_ref, mk_ref, mv_ref, wo2_ref,
                      ln2g_ref, ln2b_ref, o_ref, *, alpha):
    mix = _dot(g_ref[...], wo1_ref[0:GLA_V, :]) + _dot(d_ref[...], wo1_ref[GLA_V:, :])
    x1 = _post_ln(x_ref[...], mix, ln1g_ref[...], ln1b_ref[...], alpha)
    q = (_dot(x1.astype(BF16), wq_ref[...]) * ((MEM_DH ** -0.5) * LOG2E)).astype(BF16)
    heads = []
    for h in range(MEM_HEADS):
        cs = slice(h * MEM_DH, (h + 1) * MEM_DH)
        def head_of(ref):
            return jnp.concatenate([ref[pl.ds(_mem_row(h, j), MEM_LEN, stride=_MEM_SPLIT), :]
                                    for j in range(_MEM_PIECES)], axis=1).astype(BF16)

        s = _dot_nt(q[:, cs], head_of(mk_ref))
        p = jnp.exp2(s - jnp.max(s, axis=-1, keepdims=True))
        l = jnp.sum(p, axis=-1, keepdims=True)
        heads.append((_dot(p.astype(BF16), head_of(mv_ref)) / l).astype(BF16))
    y = _dot(jnp.concatenate(heads, axis=1), wo2_ref[...])
    o_ref[...] = _post_ln(x1, y, ln2g_ref[...], ln2b_ref[...], alpha)


def _mix_cross(g, d, x, w_out, ln1g, ln1b, wq, mk, mv, mem_layer, wo, ln2g, ln2b, tm, alpha):
    bsz, r, _ = x.shape
    row = lambda n: pl.BlockSpec((None, tm, n), lambda b, i: (b, i, 0))
    full = lambda a: pl.BlockSpec(a.shape, lambda b, i: (0,) * a.ndim)
    mem = pl.BlockSpec((None, None, MEM_LEN * _MEM_SPLIT, LANES), lambda b, i: (mem_layer, b, 0, 0))
    return pl.pallas_call(
        functools.partial(_mix_cross_kernel, alpha=alpha),
        grid=(bsz, r // tm),
        in_specs=[row(GLA_V), row(DIFF_V), row(D_MODEL), full(w_out), full(ln1g), full(ln1b), full(wq),
                  mem, mem, full(wo), full(ln2g), full(ln2b)],
        out_specs=row(D_MODEL),
        out_shape=jax.ShapeDtypeStruct(x.shape, F32),
        compiler_params=_cparams(("parallel", "parallel")),
        name="mix_cross",
    )(g, d, x, w_out, ln1g, ln1b, wq, mk, mv, wo, ln2g, ln2b)


_CARRY = 8


def _ffn_kernel(x_ref, st_ref, wua_ref, wub_ref, cw_ref, cb_ref, wd_ref, g_ref, b_ref,
                o_ref, ns_ref, carry_scr, acc_scr, *, tm, stride, n_valid, alpha):
    i = pl.program_id(1)
    f = pl.program_id(2)
    x = x_ref[...]
    xb = x.astype(BF16)
    rows = lax.broadcasted_iota(jnp.int32, (tm, 1), 0)

    if stride == 1:
        @pl.when(i == 0)
        def _():
            for half in range(2):
                carry_scr[f, half, _CARRY - 2:_CARRY - 1, :] = st_ref[half, 0]
                carry_scr[f, half, _CARRY - 1:_CARRY, :] = st_ref[half, 1]

    halves = []
    for half, w_ref in enumerate((wua_ref, wub_ref)):
        u = _dot(xb, w_ref[...])
        if stride == 1:
            p1 = carry_scr[f, half, _CARRY - 1:_CARRY, :]
            p2 = carry_scr[f, half, _CARRY - 2:_CARRY - 1, :]
            u1 = jnp.where(rows == 0, p1, pltpu.roll(u, 1, 0))
            u2 = jnp.where(rows == 0, p2, jnp.where(rows == 1, p1, pltpu.roll(u, 2, 0)))
            carry_scr[f, half] = u[tm - _CARRY:tm, :]
        else:
            padded = jnp.concatenate([st_ref[half, 0], st_ref[half, 1], u], axis=0)
            u1 = padded[stride:stride + tm]
            u2 = padded[0:tm]
        cw = cw_ref[half]
        halves.append(cb_ref[half] + cw[0:1] * u2 + cw[1:2] * u1 + cw[2:3] * u)
        for j in range(CONV_W - 1):
            r0 = (n_valid - (CONV_W - 1) + j) * stride
            ns_ref[half, j] = u[r0:r0 + stride, :]
    a, bb = halves
    hmid = (a * (1.0 / (1.0 + jnp.exp(-a))) * bb).astype(BF16)
    part = _dot(hmid, wd_ref[...])

    @pl.when(f == 0)
    def _():
        acc_scr[...] = part

    @pl.when(f > 0)
    def _():
        acc_scr[...] += part

    @pl.when(f == pl.num_programs(2) - 1)
    def _():
        o_ref[...] = _post_ln(x, acc_scr[...], g_ref[...], b_ref[...], alpha)


def _ffn(x, state, w_up, conv_w, conv_b, w_down, g, b, tm, tf, stride, n_valid, alpha):
    bsz, r, _ = x.shape
    nf = D_FF // tf
    assert stride == 1 or r == tm
    row = pl.BlockSpec((None, tm, D_MODEL), lambda bi, i, f: (bi, i, 0))
    st = pl.BlockSpec((None, 2, CONV_W - 1, stride, tf), lambda bi, i, f: (bi, 0, 0, 0, f))
    full = lambda a: pl.BlockSpec(a.shape, lambda bi, i, f: (0,) * a.ndim)
    return pl.pallas_call(
        functools.partial(_ffn_kernel, tm=tm, stride=stride, n_valid=n_valid, alpha=alpha),
        grid=(bsz, r // tm, nf),
        in_specs=[row, st,
                  pl.BlockSpec((D_MODEL, tf), lambda bi, i, f: (0, f)),
                  pl.BlockSpec((D_MODEL, tf), lambda bi, i, f: (0, nf + f)),
                  pl.BlockSpec((2, CONV_W, tf), lambda bi, i, f: (0, 0, f)),
                  pl.BlockSpec((2, 1, tf), lambda bi, i, f: (0, 0, f)),
                  pl.BlockSpec((tf, D_MODEL), lambda bi, i, f: (f, 0)),
                  full(g), full(b)],
        out_specs=[row, pl.BlockSpec((None, None, 2, CONV_W - 1, stride, tf), lambda bi, i, f: (bi, i, 0, 0, 0, f))],
        out_shape=[jax.ShapeDtypeStruct(x.shape, F32),
                   jax.ShapeDtypeStruct((bsz, r // tm, 2, CONV_W - 1, stride, D_FF), F32)],
        scratch_shapes=[pltpu.VMEM((nf, 2, _CARRY, tf), F32), pltpu.VMEM((tm, D_MODEL), F32)],
        compiler_params=_cparams(("parallel", "arbitrary", "arbitrary")),
        name="ffn",
    )(x, state, w_up, w_up, conv_w, conv_b, w_down, g, b)


def _rope_tables(pos):
    half = DIFF_DH // 2
    inv = ROPE_THETA ** (-jnp.arange(half, dtype=F32) / half)
    ang = pos.astype(F32)[:, None] * inv[None, :]
    cos = jnp.tile(jnp.cos(ang), (1, LANES // half))
    sin = jnp.tile(jnp.sin(ang), (1, LANES // half))
    first = (jnp.arange(LANES) % DIFF_DH) < half
    return cos, jnp.where(first, -sin, 0.0), jnp.where(first, 0.0, sin)


def _conv_state_in(s):
    return s.reshape(s.shape[0], CONV_W - 1, 2, D_FF).transpose(2, 1, 0, 3)[None]


def _conv_state_out(s):
    return s[0].transpose(2, 1, 0, 3).reshape(s.shape[3], CONV_W - 1, 2 * D_FF)


def kernel(x_prompt, x_sample, mem_prompt, cache_k, cache_v, page_table, cache_mem_k, cache_mem_v,
           state_gla, state_conv, w_in, gla_wa2, gla_ba, gla_norm_w, diff_lambda, diff_norm_w, w_out,
           ln1_g, ln1_b, cross_wq, cross_wk, cross_wv, cross_wo, ln2_g, ln2_b,
           ffn_w_up, ffn_conv_w, ffn_conv_b, ffn_w_down, ln3_g, ln3_b):
    depth = w_in.shape[0]
    bp, tp, _ = x_prompt.shape
    bs, ts, _ = x_sample.shape
    n_pages = page_table.shape[1]
    past_len = n_pages * PAGE_SIZE
    alpha = (2 * depth) ** 0.25
    assert ts <= _QROWS and tp % 512 == 0

    tm_a = 512
    tm_p = 512
    tq = 512
    gla_tt = 256
    ffn_tf = D_FF // 2
    pages = 8

    tabs_p = _rope_tables(jnp.arange(tp))
    pos_s = past_len + jnp.minimum(jnp.arange(SAMPLE_PAD), ts - 1)
    tabs_s = _rope_tables(jnp.tile(pos_s, bs))

    xp = x_prompt.reshape(bp * tp, D_MODEL)
    xs = jnp.pad(x_sample, ((0, 0), (0, SAMPLE_PAD - ts), (0, 0))).reshape(bs * SAMPLE_PAD, D_MODEL)
    mem2 = mem_prompt.reshape(bp * MEM_LEN, D_MODEL)
    zero_conv = jnp.zeros((bp, 2, CONV_W - 1, 1, D_FF), F32)
    n_phys = cache_k.shape[1]
    cache_k4 = cache_k.reshape(depth, n_phys, PAGE_SIZE * DIFF_HEADS, 2 * DIFF_DH)
    cache_v4 = cache_v.reshape(depth, n_phys, PAGE_SIZE * DIFF_HEADS, DIFF_DV)
    cmem_k = _mem_rows_view(cache_mem_k)
    cmem_v = _mem_rows_view(cache_mem_v)

    outs = [[] for _ in range(10)]
    for l in range(depth):
        lam_init = 0.8 - 0.6 * math.exp(-0.3 * l)
        wl = w_in[l]
        ga0 = 2 * GLA_QK + 2 * GLA_V
        w_in_l = jnp.concatenate(
            [wl[:, :ga0], wl[:, ga0 + GLA_RANK:], wl[:, ga0:ga0 + GLA_RANK],
             jnp.zeros((D_MODEL, LANES - GLA_RANK), F32)], axis=1).astype(BF16)
        wa2_l = jnp.pad(gla_wa2[l], ((0, LANES - GLA_RANK), (0, 0))).astype(BF16)
        ba_l = gla_ba[l].reshape(1, GLA_QK)
        gnw = gla_norm_w[l].reshape(1, GLA_DV)
        dnw = diff_norm_w[l].reshape(1, DIFF_DV)
        w_out_l = w_out[l].astype(BF16)
        wq_l = cross_wq[l].astype(BF16)
        wk_l = cross_wk[l].astype(BF16)
        wv_l = cross_wv[l].astype(BF16)
        wo_l = cross_wo[l].astype(BF16)
        w_up_l = ffn_w_up[l].astype(BF16)
        w_down_l = ffn_w_down[l].astype(BF16)
        cw_l = ffn_conv_w[l].reshape(CONV_W, 2, D_FF).transpose(1, 0, 2)
        cb_l = ffn_conv_b[l].reshape(2, 1, D_FF)
        r1 = lambda a: a[l].reshape(1, D_MODEL)
        ln = [r1(a) for a in (ln1_g, ln1_b, ln2_g, ln2_b, ln3_g, ln3_b)]

        gq, gk, gg, gv, gr, dq, dkf, dvf, dkb, dvb = _inproj(xp, w_in_l, wa2_l, ba_l, tabs_p, tp // tm_a, tm_a)
        b3 = lambda a: a.reshape(bp, tp, a.shape[-1])
        go, s_p = _gla(b3(gq), b3(gk), b3(gg), b3(gv), b3(gr), gnw, None, bp, gla_tt, GLA_CHUNK, GLA_SUB, None)
        do = _diff_prompt(diff_lambda[l], dnw, b3(dq), b3(dkb), b3(dvb), tq, lam_init)
        mk_p, mv_p = _memkv(mem2, wk_l, wv_l)
        x2 = _mix_cross(go, do, xp.reshape(bp, tp, D_MODEL), w_out_l, ln[0], ln[1], wq_l,
                        mk_p.reshape(1, bp, MEM_LEN * _MEM_SPLIT, LANES),
                        mv_p.reshape(1, bp, MEM_LEN * _MEM_SPLIT, LANES), 0,
                        wo_l, ln[2], ln[3], tm_p, alpha)
        x3, conv_p = _ffn(x2, zero_conv, w_up_l, cw_l, cb_l, w_down_l, ln[4], ln[5], tm_p, ffn_tf, 1, tm_p, alpha)
        xp = x3.reshape(bp * tp, D_MODEL)
        outs[0].append(dkf.reshape(bp, tp // PAGE_SIZE, PAGE_SIZE, DIFF_HEADS, 2 * DIFF_DH))
        outs[1].append(dvf.reshape(bp, tp // PAGE_SIZE, PAGE_SIZE, DIFF_HEADS, DIFF_DV))
        outs[4].append(s_p)
        outs[6].append(conv_p[:, -1].transpose(0, 2, 1, 3, 4).reshape(bp, CONV_W - 1, 2 * D_FF))
        outs[8].append(_mem_rows_unview(mk_p.reshape(bp, MEM_LEN * _MEM_SPLIT, LANES)))
        outs[9].append(_mem_rows_unview(mv_p.reshape(bp, MEM_LEN * _MEM_SPLIT, LANES)))

        gq, gk, gg, gv, gr, dq, dkf, dvf, dkb, dvb = _inproj(xs, w_in_l, wa2_l, ba_l, tabs_s, 1, bs * SAMPLE_PAD)
        s3 = lambda a: a.reshape(bs, SAMPLE_PAD, a.shape[-1])
        go, s_s = _gla(s3(gq), s3(gk), s3(gg), s3(gv), s3(gr), gnw, state_gla[l],
                       math.gcd(bs, 8), SAMPLE_PAD, SAMPLE_PAD, SAMPLE_PAD, ts)
        dq_s = s3(dq)[:, :_QROWS].astype(F32)
        do8 = _diff_sample(page_table, diff_lambda[l], dnw, dq_s, s3(dkb), s3(dvb),
                           cache_k4, cache_v4, l, pages, ts, lam_init)
        do = jnp.pad(do8, ((0, 0), (0, SAMPLE_PAD - _QROWS), (0, 0))).astype(BF16)
        x2 = _mix_cross(go, do, xs.reshape(bs, SAMPLE_PAD, D_MODEL), w_out_l, ln[0], ln[1], wq_l,
                        cmem_k, cmem_v, l, wo_l, ln[2], ln[3], SAMPLE_PAD, alpha)
        x2t = x2.transpose(1, 0, 2).reshape(1, SAMPLE_PAD * bs, D_MODEL)
        x3t, conv_s = _ffn(x2t, _conv_state_in(state_conv[l]), w_up_l, cw_l, cb_l, w_down_l, ln[4], ln[5],
                           SAMPLE_PAD * bs, ffn_tf, bs, ts, alpha)
        xs = x3t.reshape(SAMPLE_PAD, bs, D_MODEL).transpose(1, 0, 2).reshape(bs * SAMPLE_PAD, D_MODEL)
        outs[2].append(dkf.reshape(bs, SAMPLE_PAD, DIFF_HEADS, 2 * DIFF_DH)[:, :ts])
        outs[3].append(dvf.reshape(bs, SAMPLE_PAD, DIFF_HEADS, DIFF_DV)[:, :ts])
        outs[5].append(s_s)
        outs[7].append(_conv_state_out(conv_s[:, -1]))

    st = [jnp.stack(o) for o in outs]
    y_prompt = xp.reshape(bp, tp, D_MODEL)
    y_sample = xs.reshape(bs, SAMPLE_PAD, D_MODEL)[:, :ts]
    return (y_prompt, y_sample, st[0], st[1], st[2], st[3], st[4], st[5], st[6], st[7], st[8], st[9])
```

```python
import functools
import math

import jax
import jax.numpy as jnp
from jax import lax
from jax.experimental import pallas as pl
from jax.experimental.pallas import tpu as pltpu

F32 = jnp.float32
BF16 = jnp.bfloat16

D_MODEL = 1024
PAGE_SIZE = 128
GLA_HEADS = 4
GLA_DK = 64
GLA_DV = 128
GLA_RANK = 16
GLA_TAU = 16.0
GLA_CHUNK = 64
GLA_SUB = 16
DIFF_HEADS = 4
DIFF_DH = 64
DIFF_DV = 128
ROPE_THETA = 10000.0
MEM_LEN = 256
MEM_HEADS = 4
MEM_DH = D_MODEL // MEM_HEADS
D_FF = 2816
CONV_W = 3
LN_EPS = 1e-5
LOG2E = 1.4426950408889634

GLA_QK = GLA_HEADS * GLA_DK
GLA_V = GLA_HEADS * GLA_DV
DIFF_QK = DIFF_HEADS * 2 * DIFF_DH
DIFF_V = DIFF_HEADS * DIFF_DV
LANES = 128
N_IN_PAD = 2 * GLA_QK + 2 * GLA_V + 2 * DIFF_QK + DIFF_V + LANES
SAMPLE_PAD = 16
VMEM_LIMIT = 52 * 1024 * 1024


def _cparams(sem):
    return pltpu.CompilerParams(dimension_semantics=sem, vmem_limit_bytes=VMEM_LIMIT)


def _post_ln(res, sub, g, b, alpha):
    h = alpha * res + sub
    mu = jnp.mean(h, axis=-1, keepdims=True)
    hc = h - mu
    var = jnp.mean(hc * hc, axis=-1, keepdims=True)
    return hc * lax.rsqrt(var + LN_EPS) * g + b


def _dot(a, b):
    return jnp.dot(a, b, preferred_element_type=F32)


def _dot_nt(a, b):
    return lax.dot_general(a, b, (((1,), (1,)), ((), ())), preferred_element_type=F32)


def _dot_tn(a, b):
    return lax.dot_general(a, b, (((0,), (0,)), ((), ())), preferred_element_type=F32)


_O_GQ, _O_GK, _O_GV, _O_GR = 0, GLA_QK, 2 * GLA_QK, 2 * GLA_QK + GLA_V
_O_DQ = 2 * GLA_QK + 2 * GLA_V
_O_DK = _O_DQ + DIFF_QK
_O_DV = _O_DK + DIFF_QK
_O_GA = _O_DV + DIFF_V


def _inproj_kernel(x_ref, w_ref, wa2_ref, ba_ref, c_ref, sa_ref, sb_ref, *rest):
    gq_ref, gk_ref, gg_ref, gv_ref, gr_ref, dq_ref, dkf_ref, dvf_ref, dkb_ref, dvb_ref = rest[-10:]
    h = _dot(x_ref[...].astype(BF16), w_ref[...])
    gq_ref[...] = h[:, _O_GQ:_O_GQ + GLA_QK] * (GLA_DK ** -0.5)
    gk_ref[...] = h[:, _O_GK:_O_GK + GLA_QK]
    gv_ref[...] = h[:, _O_GV:_O_GV + GLA_V].astype(BF16)
    gr_ref[...] = h[:, _O_GR:_O_GR + GLA_V]
    xg = _dot(h[:, _O_GA:_O_GA + LANES].astype(BF16), wa2_ref[...]) + ba_ref[...]
    gg_ref[...] = (jnp.minimum(xg, 0.0) - jnp.log1p(jnp.exp(-jnp.abs(xg)))) * (1.0 / GLA_TAU)
    c, sa, sb = c_ref[...], sa_ref[...], sb_ref[...]

    def rope(v):
        return v * c + pltpu.roll(v, LANES - DIFF_DH // 2, 1) * sa + pltpu.roll(v, DIFF_DH // 2, 1) * sb

    qscale = (DIFF_DH ** -0.5) * LOG2E
    tm = h.shape[0]
    for j in range(DIFF_HEADS):
        sl = slice(j * LANES, (j + 1) * LANES)
        head_rows = pl.ds(j, tm, stride=DIFF_HEADS)
        dq_ref[:, sl] = (rope(h[:, _O_DQ + j * LANES:_O_DQ + (j + 1) * LANES]) * qscale).astype(BF16)
        rk = rope(h[:, _O_DK + j * LANES:_O_DK + (j + 1) * LANES])
        dkf_ref[head_rows, :] = rk
        dkb_ref[:, sl] = rk.astype(BF16)
        dv = h[:, _O_DV + j * LANES:_O_DV + (j + 1) * LANES]
        dvf_ref[head_rows, :] = dv
        dvb_ref[:, sl] = dv.astype(BF16)


_I_DKF, _I_DVF = 6, 7


def _inproj(x, w, wa2, ba, tabs, tab_blocks, tm, stack=None):
    m = x.shape[0]
    row = lambda n: pl.BlockSpec((tm, n), lambda i: (i, 0))
    full = lambda a: pl.BlockSpec(a.shape, lambda i: (0,) * a.ndim)
    tab = pl.BlockSpec((tm, LANES), lambda i: (i % tab_blocks, 0))
    shapes = [(1, GLA_QK, F32), (1, GLA_QK, F32), (1, GLA_QK, F32), (1, GLA_V, BF16), (1, GLA_V, F32),
              (1, DIFF_QK, BF16), (DIFF_HEADS, LANES, F32), (DIFF_HEADS, LANES, F32),
              (1, DIFF_QK, BF16), (1, DIFF_V, BF16)]
    out_specs = [pl.BlockSpec((tm * k, n), lambda i: (i, 0)) for k, n, _ in shapes]
    out_shape = [jax.ShapeDtypeStruct((m * k, n), dt) for k, n, dt in shapes]
    in_specs = [row(D_MODEL), full(w), full(wa2), full(ba), tab, tab, tab]
    args = [x, w, wa2, ba, *tabs]
    aliases = {}
    if stack is not None:
        layer, depth, kbuf, vbuf = stack
        for idx, buf in ((_I_DKF, kbuf), (_I_DVF, vbuf)):
            k, n, dt = shapes[idx]
            out_specs[idx] = pl.BlockSpec((None, tm * k, n), lambda i: (layer, i, 0))
            out_shape[idx] = jax.ShapeDtypeStruct((depth, m * k, n), dt)
            if buf is not None:
                aliases[len(args)] = idx
                in_specs.append(pl.BlockSpec(memory_space=pl.ANY))
                args.append(buf)
    return pl.pallas_call(
        _inproj_kernel,
        grid=(m // tm,),
        in_specs=in_specs,
        out_specs=out_specs,
        out_shape=out_shape,
        input_output_aliases=aliases,
        compiler_params=_cparams(("parallel",)),
        name="inproj",
    )(*args)


def _gla_chunk(qc, kc, gc, vc, st, tril, head_a, st_mask, chunk, sub):
    nsub = chunk // sub
    g1 = gc.astype(BF16)
    r1 = gc - g1.astype(F32)
    g2 = r1.astype(BF16)
    g3 = (r1 - g2.astype(F32)).astype(BF16)
    b3 = _dot(tril, jnp.concatenate([g1, g2, g3], axis=1))
    b = b3[:, 0:LANES] + b3[:, LANES:2 * LANES] + b3[:, 2 * LANES:]
    b_last = b[chunk - 1:chunk, :]

    def by_head(x):
        return jnp.concatenate([jnp.where(head_a, x, 0.0), jnp.where(head_a, 0.0, x)], axis=0).astype(BF16)

    oi = _dot_nt(by_head(qc * jnp.exp(b)), st.astype(BF16))
    o_inter = jnp.concatenate([oi[0:chunk, 0:GLA_DV], oi[chunk:, GLA_DV:]], axis=1)
    key_row = lax.broadcasted_iota(jnp.int32, (chunk, 1), 0)
    lqs, kss = [], []
    for s in range(nsub):
        r0, r1_ = s * sub, (s + 1) * sub
        anchor = b[r0:r0 + 1, :]
        lqs.append(by_head(qc[r0:r1_] * jnp.exp(b[r0:r1_] - anchor)))
        kss.append((kc * jnp.exp(jnp.where(key_row < r1_, anchor - b, 0.0))).astype(BF16))
    att = lax.dot_general(jnp.stack(lqs), jnp.stack(kss), (((2,), (2,)), ((0,), (0,))),
                          preferred_element_type=F32)
    q_tok = (lax.broadcasted_iota(jnp.int32, att.shape, 0) * sub
             + lax.broadcasted_iota(jnp.int32, att.shape, 1) % sub)
    att = jnp.where(lax.broadcasted_iota(jnp.int32, att.shape, 2) <= q_tok, att, 0.0)
    ov = _dot(att.reshape(nsub * 2 * sub, chunk).astype(BF16), vc)
    o_intra = jnp.concatenate(
        [jnp.concatenate([ov[2 * s * sub:(2 * s + 1) * sub, 0:GLA_DV],
                          ov[(2 * s + 1) * sub:(2 * s + 2) * sub, GLA_DV:]], axis=1) for s in range(nsub)], axis=0)
    upd = _dot_tn(vc, (kc * jnp.exp(b_last - b)).astype(BF16))
    return o_inter + o_intra, st * jnp.exp(b_last) + jnp.where(st_mask, upd, 0.0)


def _gla_kernel(*refs, bb, tt, chunk, sub, t_valid, has_s0):
    if has_s0:
        q_ref, k_ref, g_ref, v_ref, gr_ref, nw_ref, s0_ref, o_ref, s_out_ref, st_scr = refs
    else:
        q_ref, k_ref, g_ref, v_ref, gr_ref, nw_ref, o_ref, s_out_ref, st_scr = refs
    i = pl.program_id(1)
    pairs = GLA_HEADS // 2
    head_a = lax.broadcasted_iota(jnp.int32, (1, LANES), 1) < GLA_DK
    st_mask = (lax.broadcasted_iota(jnp.int32, (2 * GLA_DV, LANES), 0) // GLA_DV
               == lax.broadcasted_iota(jnp.int32, (2 * GLA_DV, LANES), 1) // GLA_DK)
    tril = (lax.broadcasted_iota(jnp.int32, (chunk, chunk), 0)
            >= lax.broadcasted_iota(jnp.int32, (chunk, chunk), 1)).astype(BF16)
    row_in_chunk = lax.broadcasted_iota(jnp.int32, (chunk, 1), 0)

    @pl.when(i == 0)
    def _():
        for bi in range(bb):
            for p in range(pairs):
                if has_s0:
                    s0t = s0_ref[bi, 2 * p:2 * p + 2].reshape(2 * GLA_DK, GLA_DV).T
                    st_scr[bi, p] = jnp.where(st_mask, jnp.concatenate([s0t, s0t], axis=0), 0.0)
                else:
                    st_scr[bi, p] = jnp.zeros((2 * GLA_DV, LANES), F32)

    nw = nw_ref[...]
    for bi in range(bb):
        for p in range(pairs):
            ks = slice(p * LANES, (p + 1) * LANES)
            vs = slice(p * 2 * GLA_DV, (p + 1) * 2 * GLA_DV)
            st = st_scr[bi, p]
            for c in range(tt // chunk):
                rs = slice(c * chunk, (c + 1) * chunk)
                qc, kc, gc, vc = q_ref[bi, rs, ks], k_ref[bi, rs, ks], g_ref[bi, rs, ks], v_ref[bi, rs, vs]
                if t_valid is not None:
                    ok = (row_in_chunk + c * chunk) < t_valid
                    kc = jnp.where(ok, kc, 0.0)
                    gc = jnp.where(ok, gc, 0.0)
                    vc = jnp.where(ok, vc, jnp.zeros_like(vc))
                o, st = _gla_chunk(qc, kc, gc, vc, st, tril, head_a, st_mask, chunk, sub)
                gr = gr_ref[bi, rs, vs]
                normed = []
                for hh in range(2):
                    oh = o[:, hh * GLA_DV:(hh + 1) * GLA_DV]
                    normed.append(oh * lax.rsqrt(jnp.mean(oh * oh, axis=-1, keepdims=True) + LN_EPS) * nw)
                gate = gr * (1.0 / (1.0 + jnp.exp(-gr)))
                o_ref[bi, rs, vs] = (jnp.concatenate(normed, axis=1) * gate).astype(o_ref.dtype)
            st_scr[bi, p] = st

    @pl.when(i == pl.num_programs(1) - 1)
    def _():
        for bi in range(bb):
            for p in range(pairs):
                st = st_scr[bi, p]
                sa = st[0:GLA_DV].T
                sb = st[GLA_DV:].T
                s_out_ref[bi, 2 * p:2 * p + 2] = jnp.concatenate(
                    [sa[0:GLA_DK], sb[GLA_DK:]], axis=0).reshape(2, GLA_DK, GLA_DV)


def _gla(q, k, g, v, gr, nw, s0, bb, tt, chunk, sub, t_valid):
    bsz, t, _ = q.shape
    has_s0 = s0 is not None
    qk_spec = pl.BlockSpec((bb, tt, GLA_QK), lambda b, i: (b, i, 0))
    v_spec = pl.BlockSpec((bb, tt, GLA_V), lambda b, i: (b, i, 0))
    s_spec = pl.BlockSpec((bb, GLA_HEADS, GLA_DK, GLA_DV), lambda b, i: (b, 0, 0, 0))
    in_specs = [qk_spec, qk_spec, qk_spec, v_spec, v_spec, pl.BlockSpec(nw.shape, lambda b, i: (0, 0))]
    args = [q, k, g, v, gr, nw]
    if has_s0:
        in_specs.append(s_spec)
        args.append(s0)
    return pl.pallas_call(
        functools.partial(_gla_kernel, bb=bb, tt=tt, chunk=chunk, sub=sub, t_valid=t_valid, has_s0=has_s0),
        grid=(bsz // bb, t // tt),
        in_specs=in_specs,
        out_specs=[v_spec, s_spec],
        out_shape=[jax.ShapeDtypeStruct((bsz, t, GLA_V), BF16),
                   jax.ShapeDtypeStruct((bsz, GLA_HEADS, GLA_DK, GLA_DV), F32)],
        scratch_shapes=[pltpu.VMEM((bb, GLA_HEADS // 2, 2 * GLA_DV, LANES), F32)],
        compiler_params=_cparams(("parallel", "arbitrary")),
        name="gla",
    )(*args)


def _diff_lambda(lam_ref, lam_init):
    lf = lam_ref[...]
    a = jnp.sum(lf[0:1] * lf[1:2], axis=-1, keepdims=True)
    b = jnp.sum(lf[2:3] * lf[3:4], axis=-1, keepdims=True)
    return jnp.exp(a) - jnp.exp(b) + lam_init


def _diff_finish(o0, o1, lam, nw, lam_init):
    od = o0 - lam * o1
    ms = jnp.mean(od * od, axis=-1, keepdims=True)
    return od * lax.rsqrt(ms + LN_EPS) * nw * (1.0 - lam_init)


def _diff_prompt_kernel(lam_ref, nw_ref, q_ref, k_ref, v_ref, o_ref,
                        qz_scr, s_scr, p_scr, m_scr, a_scr, acc_scr, *, tq, rc, lam_init):
    qi = pl.program_id(2)
    q = q_ref[...]
    lane = lax.broadcasted_iota(jnp.int32, (1, LANES), 1)
    zero = jnp.zeros_like(q)
    qz_scr[0:tq, :] = jnp.where(lane < DIFF_DH, q, zero)
    qz_scr[tq:, :] = jnp.where(lane < DIFF_DH, zero, q)
    m_scr[...] = jnp.full_like(m_scr, -jnp.inf)
    acc_scr[...] = jnp.zeros_like(acc_scr)
    p_scr[1] = jnp.zeros(p_scr.shape[1:], BF16)
    a_scr[1] = jnp.ones(a_scr.shape[1:], F32)
    ones = jnp.ones((tq, LANES), BF16)

    def scores(t, slot):
        ks = pl.multiple_of(t * tq, tq)
        s_scr[slot] = _dot_nt(qz_scr[...], k_ref[pl.ds(ks, tq), :])

    def softmax(slot, masked):
        for c in range(2 * tq // rc):
            rows = slice(c * rc, (c + 1) * rc)
            s = s_scr[slot, rows, :]
            if masked:
                rr = lax.broadcasted_iota(jnp.int32, (rc, 1), 0) + (c * rc) % tq
                cc = lax.broadcasted_iota(jnp.int32, (1, tq), 1)
                s = jnp.where(cc <= rr, s, -jnp.inf)
            m_prev = m_scr[rows, :]
            m_new = jnp.maximum(m_prev, jnp.max(s, axis=1, keepdims=True))
            a_scr[slot, rows, :] = jnp.exp2(m_prev - m_new)
            m_scr[rows, :] = m_new
            p_scr[slot, rows, :] = jnp.exp2(s - jnp.tile(m_new, (1, tq // LANES))).astype(BF16)

    def values(t, slot):
        ks = pl.multiple_of(t * tq, tq)
        v1 = jnp.concatenate([v_ref[pl.ds(ks, tq), :], ones], axis=1)
        a = a_scr[slot]
        acc_scr[...] = acc_scr[...] * jnp.concatenate([a, a], axis=1) + _dot(p_scr[slot], v1)

    scores(0, 0)

    def step(t, slot):
        scores(t + 1, 1 - slot)
        softmax(slot, False)
        values(jnp.maximum(t - 1, 0), 1 - slot)

    def body(u, carry):
        step(2 * u, 0)
        step(2 * u + 1, 1)
        return carry

    lax.fori_loop(0, qi // 2, body, 0)

    def tail(slot):
        softmax(slot, True)
        values(jnp.maximum(qi - 1, 0), 1 - slot)
        values(qi, slot)

    @pl.when(qi % 2 == 0)
    def _():
        tail(0)

    @pl.when(qi % 2 == 1)
    def _():
        softmax(0, False)
        values(jnp.maximum(qi - 2, 0), 1)
        values(qi - 1, 0)
        scores(qi, 1)
        softmax(1, True)
        values(qi, 1)

    acc = acc_scr[...]
    o0 = acc[0:tq, 0:DIFF_DV] / acc[0:tq, DIFF_DV:DIFF_DV + 1]
    o1 = acc[tq:, 0:DIFF_DV] / acc[tq:, DIFF_DV:DIFF_DV + 1]
    lam = _diff_lambda(lam_ref, lam_init)
    o_ref[...] = _diff_finish(o0, o1, lam, nw_ref[...], lam_init).astype(o_ref.dtype)


def _diff_prompt(lam_p, nw, q, k, v, tq, lam_init):
    bsz, t, _ = q.shape
    qspec = pl.BlockSpec((None, tq, LANES), lambda b, h, i: (b, i, h))
    kvspec = pl.BlockSpec((None, t, LANES), lambda b, h, i: (b, 0, h))
    small = lambda a: pl.BlockSpec(a.shape, lambda b, h, i: (0,) * a.ndim)
    return pl.pallas_call(
        functools.partial(_diff_prompt_kernel, tq=tq, rc=32, lam_init=lam_init),
        grid=(bsz, DIFF_HEADS, t // tq),
        in_specs=[small(lam_p), small(nw), qspec, kvspec, kvspec],
        out_specs=qspec,
        out_shape=jax.ShapeDtypeStruct((bsz, t, DIFF_V), BF16),
        scratch_shapes=[pltpu.VMEM((2 * tq, LANES), BF16),
                        pltpu.VMEM((2, 2 * tq, tq), F32),
                        pltpu.VMEM((2, 2 * tq, tq), BF16),
                        pltpu.VMEM((2 * tq, LANES), F32),
                        pltpu.VMEM((2, 2 * tq, LANES), F32),
                        pltpu.VMEM((2 * tq, 2 * LANES), F32)],
        compiler_params=_cparams(("parallel", "parallel", "arbitrary")),
        name="diff_prompt",
    )(lam_p, nw, q, k, v)


_QROWS = 8


def _diff_sample_kernel(pt_ref, lam_ref, nw_ref, q_ref, kn_ref, vn_ref, *rest, pages, t_new, lam_init):
    k_refs, v_refs = rest[:pages], rest[pages:2 * pages]
    o_ref, qz_scr, m_scr, l_scr, acc_scr = rest[2 * pages:]
    j = pl.program_id(1)
    grp = 2 * _QROWS

    def update(keys, vals, mask):
        s = jnp.concatenate([_dot_nt(qz_scr[h * grp:(h + 1) * grp, :], keys[h]) for h in range(DIFF_HEADS)], axis=0)
        if mask is not None:
            s = jnp.where(mask, s, -jnp.inf)
        m_prev = m_scr[:, 0:1]
        m_new = jnp.maximum(m_prev, jnp.max(s, axis=1, keepdims=True))
        alpha = jnp.exp2(m_prev - m_new)
        p = jnp.exp2(s - m_new)
        l_new = l_scr[:, 0:1] * alpha + jnp.sum(p, axis=1, keepdims=True)
        pb = p.astype(BF16)
        pv = jnp.concatenate([_dot(pb[h * grp:(h + 1) * grp], vals[h]) for h in range(DIFF_HEADS)], axis=0)
        l_scr[...] = jnp.broadcast_to(l_new, l_scr.shape)
        acc_scr[...] = acc_scr[...] * alpha + pv
        m_scr[...] = jnp.broadcast_to(m_new, m_scr.shape)

    @pl.when(j == 0)
    def _():
        q = q_ref[...]
        lane = lax.broadcasted_iota(jnp.int32, (1, LANES), 1)
        m_scr[...] = jnp.full_like(m_scr, -jnp.inf)
        l_scr[...] = jnp.zeros_like(l_scr)
        acc_scr[...] = jnp.zeros_like(acc_scr)
        for h in range(DIFF_HEADS):
            qh = q[:, h * LANES:(h + 1) * LANES]
            qz_scr[h * grp:(h + 1) * grp, :] = jnp.concatenate(
                [jnp.where(lane < DIFF_DH, qh, 0.0), jnp.where(lane < DIFF_DH, 0.0, qh)], axis=0).astype(BF16)
        rr = lax.broadcasted_iota(jnp.int32, (DIFF_HEADS * grp, SAMPLE_PAD), 0) % _QROWS
        cc = lax.broadcasted_iota(jnp.int32, (DIFF_HEADS * grp, SAMPLE_PAD), 1)
        update([kn_ref[:, h * LANES:(h + 1) * LANES] for h in range(DIFF_HEADS)],
               [vn_ref[:, h * LANES:(h + 1) * LANES] for h in range(DIFF_HEADS)], (cc <= rr) & (cc < t_new))

    def head_of(refs, h):
        return jnp.concatenate([r[pl.ds(h, PAGE_SIZE, stride=DIFF_HEADS), :].astype(BF16) for r in refs], axis=0)

    update([head_of(k_refs, h) for h in range(DIFF_HEADS)], [head_of(v_refs, h) for h in range(DIFF_HEADS)], None)

    @pl.when(j == pl.num_programs(1) - 1)
    def _():
        acc = acc_scr[...]
        inv = 1.0 / l_scr[:, 0:1]
        lam = _diff_lambda(lam_ref, lam_init)
        nw = nw_ref[...]
        outs = []
        for h in range(DIFF_HEADS):
            r0 = h * grp
            r1 = r0 + _QROWS
            o0 = acc[r0:r0 + _QROWS] * inv[r0:r0 + _QROWS]
            o1 = acc[r1:r1 + _QROWS] * inv[r1:r1 + _QROWS]
            outs.append(_diff_finish(o0, o1, lam, nw, lam_init))
        o_ref[...] = jnp.concatenate(outs, axis=1)


def _diff_sample(page_table, lam_p, nw, q, k_new, v_new, cache_k, cache_v, layer, pages, t_new, lam_init):
    bsz, n_pages = page_table.shape
    small = lambda a: pl.BlockSpec(a.shape, lambda b, j, pt: (0,) * a.ndim)
    qspec = pl.BlockSpec((None, _QROWS, DIFF_QK), lambda b, j, pt: (b, 0, 0))
    nspec = pl.BlockSpec((None, SAMPLE_PAD, DIFF_QK), lambda b, j, pt: (b, 0, 0))

    def page_spec(i):
        return pl.BlockSpec((None, None, PAGE_SIZE * DIFF_HEADS, LANES),
                            lambda b, j, pt: (layer, pt[b, j * pages + i], 0, 0))

    nrow = 2 * DIFF_HEADS * _QROWS
    grid_spec = pltpu.PrefetchScalarGridSpec(
        num_scalar_prefetch=1,
        grid=(bsz, n_pages // pages),
        in_specs=[small(lam_p), small(nw), qspec, nspec, nspec]
                 + [page_spec(i) for i in range(pages)] + [page_spec(i) for i in range(pages)],
        out_specs=qspec,
        scratch_shapes=[pltpu.VMEM((nrow, LANES), BF16), pltpu.VMEM((nrow, LANES), F32),
                        pltpu.VMEM((nrow, LANES), F32), pltpu.VMEM((nrow, DIFF_DV), F32)],
    )
    return pl.pallas_call(
        functools.partial(_diff_sample_kernel, pages=pages, t_new=t_new, lam_init=lam_init),
        grid_spec=grid_spec,
        out_shape=jax.ShapeDtypeStruct((bsz, _QROWS, DIFF_V), F32),
        compiler_params=_cparams(("parallel", "arbitrary")),
        name="diff_sample",
    )(page_table, lam_p, nw, q, k_new, v_new, *([cache_k] * pages), *([cache_v] * pages))


_MEM_SPLIT = D_MODEL // LANES
_MEM_PIECES = MEM_DH // LANES


def _mem_row(head, piece):
    return piece * MEM_HEADS + head


def _mem_rows_view(a):
    lead = a.shape[:-3]
    n = len(lead)
    a = a.reshape(*lead, MEM_LEN, MEM_HEADS, _MEM_PIECES, LANES)
    a = a.transpose(*range(n), n, n + 2, n + 1, n + 3)
    return a.reshape(*lead, MEM_LEN * _MEM_SPLIT, LANES)


def _mem_rows_unview(a):
    lead = a.shape[:-2]
    n = len(lead)
    a = a.reshape(*lead, MEM_LEN, _MEM_PIECES, MEM_HEADS, LANES)
    a = a.transpose(*range(n), n, n + 2, n + 1, n + 3)
    return a.reshape(*lead, MEM_LEN, MEM_HEADS, MEM_DH)


def _memkv_kernel(x_ref, wk_ref, wv_ref, k_ref, v_ref):
    x = x_ref[...].astype(BF16)
    m = x.shape[0]
    for w_ref, o_ref in ((wk_ref, k_ref), (wv_ref, v_ref)):
        y = _dot(x, w_ref[...])
        for j in range(_MEM_SPLIT):
            o_ref[pl.ds(_mem_row(j // _MEM_PIECES, j % _MEM_PIECES), m, stride=_MEM_SPLIT), :] = (
                y[:, j * LANES:(j + 1) * LANES])


def _memkv(mem, wk, wv):
    m = mem.shape[0]
    full = lambda a: pl.BlockSpec(a.shape, lambda i: (0, 0))
    out = pl.BlockSpec((m * _MEM_SPLIT, LANES), lambda i: (0, 0))
    return pl.pallas_call(
        _memkv_kernel,
        grid=(1,),
        in_specs=[full(mem), full(wk), full(wv)],
        out_specs=[out, out],
        out_shape=[jax.ShapeDtypeStruct((m * _MEM_SPLIT, LANES), F32)] * 2,
        compiler_params=_cparams(("arbitrary",)),
        name="memkv",
    )(mem, wk, wv)


def _mix_proj(g_ref, d_ref, x_ref, wo1_ref, ln1g_ref, ln1b_ref, wq_ref, alpha):
    mix = _dot(g_ref[...], wo1_ref[0:GLA_V, :]) + _dot(d_ref[...], wo1_ref[GLA_V:, :])
    x1 = _post_ln(x_ref[...], mix, ln1g_ref[...], ln1b_ref[...], alpha)
    return x1, (_dot(x1.astype(BF16), wq_ref[...]) * ((MEM_DH ** -0.5) * LOG2E)).astype(BF16)


def _cross_heads(q, mk_ref, mv_ref):
    heads = []
    for h in range(MEM_HEADS):
        def head_of(ref):
            return jnp.concatenate([ref[pl.ds(_mem_row(h, j), MEM_LEN, stride=_MEM_SPLIT), :]
                                    for j in range(_MEM_PIECES)], axis=1).astype(BF16)

        s = _dot_nt(q[:, h * MEM_DH:(h + 1) * MEM_DH], head_of(mk_ref))
        p = jnp.exp2(s - jnp.max(s, axis=-1, keepdims=True))
        l = jnp.sum(p, axis=-1, keepdims=True)
        heads.append((_dot(p.astype(BF16), head_of(mv_ref)) / l).astype(BF16))
    return jnp.concatenate(heads, axis=1)


def _mix_cross_kernel(g_ref, d_ref, x_ref, wo1_ref, ln1g_ref, ln1b_ref, wq_ref, mk_ref, mv_ref, wo2_ref,
                      ln2g_ref, ln2b_ref, o_ref, *, alpha):
    x1, q = _mix_proj(g_ref, d_ref, x_ref, wo1_ref, ln1g_ref, ln1b_ref, wq_ref, alpha)
    y = _dot(_cross_heads(q, mk_ref, mv_ref), wo2_ref[...])
    o_ref[...] = _post_ln(x1, y, ln2g_ref[...], ln2b_ref[...], alpha)


def _mix_proj_kernel(g_ref, d_ref, x_ref, wo1_ref, ln1g_ref, ln1b_ref, wq_ref, x1_ref, q_ref, *, alpha):
    x1_ref[...], q_ref[...] = _mix_proj(g_ref, d_ref, x_ref, wo1_ref, ln1g_ref, ln1b_ref, wq_ref, alpha)


def _cross_kernel(q_ref, mk_ref, mv_ref, o_ref):
    o_ref[...] = _cross_heads(q_ref[...], mk_ref, mv_ref)


def _out_proj_kernel(o_ref, x1_ref, wo2_ref, ln2g_ref, ln2b_ref, y_ref, *, alpha):
    y_ref[...] = _post_ln(x1_ref[...], _dot(o_ref[...], wo2_ref[...]), ln2g_ref[...], ln2b_ref[...], alpha)


def _mix_cross_split(g, d, x, w_out, ln1g, ln1b, wq, mk, mv, mem_layer, wo, ln2g, ln2b, alpha):
    bsz, r, _ = x.shape
    rows = bsz * r
    flat = lambda a: a.reshape(rows, a.shape[-1])
    full = lambda a: pl.BlockSpec(a.shape, lambda *_: (0,) * a.ndim)
    x1, q = pl.pallas_call(
        functools.partial(_mix_proj_kernel, alpha=alpha),
        grid=(1,),
        in_specs=[full(flat(g)), full(flat(d)), full(flat(x)), full(w_out), full(ln1g), full(ln1b), full(wq)],
        out_specs=[full(flat(x)), full(flat(x))],
        out_shape=[jax.ShapeDtypeStruct((rows, D_MODEL), F32), jax.ShapeDtypeStruct((rows, D_MODEL), BF16)],
        compiler_params=_cparams(("arbitrary",)),
        name="mix_proj",
    )(flat(g), flat(d), flat(x), w_out, ln1g, ln1b, wq)
    seq = pl.BlockSpec((None, r, D_MODEL), lambda b: (b, 0, 0))
    mem = pl.BlockSpec((None, None, MEM_LEN * _MEM_SPLIT, LANES), lambda b: (mem_layer, b, 0, 0))
    o = pl.pallas_call(
        _cross_kernel,
        grid=(bsz,),
        in_specs=[seq, mem, mem],
        out_specs=seq,
        out_shape=jax.ShapeDtypeStruct((bsz, r, D_MODEL), BF16),
        compiler_params=_cparams(("parallel",)),
        name="cross",
    )(q.reshape(bsz, r, D_MODEL), mk, mv)
    y = pl.pallas_call(
        functools.partial(_out_proj_kernel, alpha=alpha),
        grid=(1,),
        in_specs=[full(flat(x)), full(flat(x)), full(wo), full(ln2g), full(ln2b)],
        out_specs=full(flat(x)),
        out_shape=jax.ShapeDtypeStruct((rows, D_MODEL), F32),
        compiler_params=_cparams(("arbitrary",)),
        name="out_proj",
    )(flat(o), x1, wo, ln2g, ln2b)
    return y.reshape(bsz, r, D_MODEL)


def _mix_cross(g, d, x, w_out, ln1g, ln1b, wq, mk, mv, mem_layer, wo, ln2g, ln2b, tm, alpha):
    bsz, r, _ = x.shape
    row = lambda n: pl.BlockSpec((None, tm, n), lambda b, i: (b, i, 0))
    full = lambda a: pl.BlockSpec(a.shape, lambda b, i: (0,) * a.ndim)
    mem = pl.BlockSpec((None, None, MEM_LEN * _MEM_SPLIT, LANES), lambda b, i: (mem_layer, b, 0, 0))
    return pl.pallas_call(
        functools.partial(_mix_cross_kernel, alpha=alpha),
        grid=(bsz, r // tm),
        in_specs=[row(GLA_V), row(DIFF_V), row(D_MODEL), full(w_out), full(ln1g), full(ln1b), full(wq),
                  mem, mem, full(wo), full(ln2g), full(ln2b)],
        out_specs=row(D_MODEL),
        out_shape=jax.ShapeDtypeStruct(x.shape, F32),
        compiler_params=_cparams(("parallel", "parallel")),
        name="mix_cross",
    )(g, d, x, w_out, ln1g, ln1b, wq, mk, mv, wo, ln2g, ln2b)


_CARRY = 8


def _ffn_kernel(x_ref, st_ref, wua_ref, wub_ref, cw_ref, cb_ref, wd_ref, g_ref, b_ref,
                o_ref, ns_ref, carry_scr, acc_scr, *, tm, stride, n_valid, alpha):
    i = pl.program_id(1)
    f = pl.program_id(2)
    x = x_ref[...]
    xb = x.astype(BF16)
    rows8 = lax.broadcasted_iota(jnp.int32, (_CARRY, 1), 0)

    if stride == 1:
        @pl.when(i == 0)
        def _():
            for half in range(2):
                carry_scr[f, half, _CARRY - 2:_CARRY - 1, :] = st_ref[half, 0]
                carry_scr[f, half, _CARRY - 1:_CARRY, :] = st_ref[half, 1]

    halves = []
    for half, w_ref in enumerate((wua_ref, wub_ref)):
        u = _dot(xb, w_ref[...])
        if stride == 1:
            p1 = carry_scr[f, half, _CARRY - 1:_CARRY, :]
            p2 = carry_scr[f, half, _CARRY - 2:_CARRY - 1, :]
            r1, r2 = pltpu.roll(u, 1, 0), pltpu.roll(u, 2, 0)
            u1 = jnp.concatenate([jnp.where(rows8 == 0, p1, r1[0:_CARRY]), r1[_CARRY:]], axis=0)
            u2 = jnp.concatenate(
                [jnp.where(rows8 == 0, p2, jnp.where(rows8 == 1, p1, r2[0:_CARRY])), r2[_CARRY:]], axis=0)
            carry_scr[f, half] = u[tm - _CARRY:tm, :]
        else:
            padded = jnp.concatenate([st_ref[half, 0], st_ref[half, 1], u], axis=0)
            u1 = padded[stride:stride + tm]
            u2 = padded[0:tm]
        cw = cw_ref[half]
        halves.append(cb_ref[half] + cw[0:1] * u2 + cw[1:2] * u1 + cw[2:3] * u)
        for j in range(CONV_W - 1):
            r0 = (n_valid - (CONV_W - 1) + j) * stride
            ns_ref[half, j] = u[r0:r0 + stride, :]
    a, bb = halves
    hmid = (a * (1.0 / (1.0 + jnp.exp(-a))) * bb).astype(BF16)
    part = _dot(hmid, wd_ref[...])

    @pl.when(f == 0)
    def _():
        acc_scr[...] = part

    @pl.when(f > 0)
    def _():
        acc_scr[...] += part

    @pl.when(f == pl.num_programs(2) - 1)
    def _():
        o_ref[...] = _post_ln(x, acc_scr[...], g_ref[...], b_ref[...], alpha)


def _ffn(x, state, w_up, conv_w, conv_b, w_down, g, b, tm, tf, stride, n_valid, alpha):
    bsz, r, _ = x.shape
    nf = D_FF // tf
    assert stride == 1 or r == tm
    row = pl.BlockSpec((None, tm, D_MODEL), lambda bi, i, f: (bi, i, 0))
    st = pl.BlockSpec((None, 2, CONV_W - 1, stride, tf), lambda bi, i, f: (bi, 0, 0, 0, f))
    full = lambda a: pl.BlockSpec(a.shape, lambda bi, i, f: (0,) * a.ndim)
    return pl.pallas_call(
        functools.partial(_ffn_kernel, tm=tm, stride=stride, n_valid=n_valid, alpha=alpha),
        grid=(bsz, r // tm, nf),
        in_specs=[row, st,
                  pl.BlockSpec((D_MODEL, tf), lambda bi, i, f: (0, f)),
                  pl.BlockSpec((D_MODEL, tf), lambda bi, i, f: (0, nf + f)),
                  pl.BlockSpec((2, CONV_W, tf), lambda bi, i, f: (0, 0, f)),
                  pl.BlockSpec((2, 1, tf), lambda bi, i, f: (0, 0, f)),
                  pl.BlockSpec((tf, D_MODEL), lambda bi, i, f: (f, 0)),
                  full(g), full(b)],
        out_specs=[row, pl.BlockSpec((None, None, 2, CONV_W - 1, stride, tf), lambda bi, i, f: (bi, i, 0, 0, 0, f))],
        out_shape=[jax.ShapeDtypeStruct(x.shape, F32),
                   jax.ShapeDtypeStruct((bsz, r // tm, 2, CONV_W - 1, stride, D_FF), F32)],
        scratch_shapes=[pltpu.VMEM((nf, 2, _CARRY, tf), F32), pltpu.VMEM((tm, D_MODEL), F32)],
        compiler_params=_cparams(("parallel", "arbitrary", "arbitrary")),
        name="ffn",
    )(x, state, w_up, w_up, conv_w, conv_b, w_down, g, b)


def _rope_tables(pos):
    half = DIFF_DH // 2
    inv = ROPE_THETA ** (-jnp.arange(half, dtype=F32) / half)
    ang = pos.astype(F32)[:, None] * inv[None, :]
    cos = jnp.tile(jnp.cos(ang), (1, LANES // half))
    sin = jnp.tile(jnp.sin(ang), (1, LANES // half))
    first = (jnp.arange(LANES) % DIFF_DH) < half
    return cos, jnp.where(first, -sin, 0.0), jnp.where(first, 0.0, sin)


def _conv_state_in(s):
    return s.reshape(s.shape[0], CONV_W - 1, 2, D_FF).transpose(2, 1, 0, 3)[None]


def _conv_state_out(s):
    return s[0].transpose(2, 1, 0, 3).reshape(s.shape[3], CONV_W - 1, 2 * D_FF)


def kernel(x_prompt, x_sample, mem_prompt, cache_k, cache_v, page_table, cache_mem_k, cache_mem_v,
           state_gla, state_conv, w_in, gla_wa2, gla_ba, gla_norm_w, diff_lambda, diff_norm_w, w_out,
           ln1_g, ln1_b, cross_wq, cross_wk, cross_wv, cross_wo, ln2_g, ln2_b,
           ffn_w_up, ffn_conv_w, ffn_conv_b, ffn_w_down, ln3_g, ln3_b):
    depth = w_in.shape[0]
    bp, tp, _ = x_prompt.shape
    bs, ts, _ = x_sample.shape
    n_pages = page_table.shape[1]
    past_len = n_pages * PAGE_SIZE
    alpha = (2 * depth) ** 0.25
    assert ts <= _QROWS and tp % 512 == 0

    tm_a = 512
    tm_p = 512
    tq = 512
    gla_tt = 256
    ffn_tf = D_FF // 2
    pages = math.gcd(n_pages, 16)

    tabs_p = _rope_tables(jnp.arange(tp))
    pos_s = past_len + jnp.minimum(jnp.arange(SAMPLE_PAD), ts - 1)
    tabs_s = _rope_tables(jnp.tile(pos_s, bs))

    xp = x_prompt.reshape(bp * tp, D_MODEL)
    xs = jnp.pad(x_sample, ((0, 0), (0, SAMPLE_PAD - ts), (0, 0))).reshape(bs * SAMPLE_PAD, D_MODEL)
    mem2 = mem_prompt.reshape(bp * MEM_LEN, D_MODEL)
    zero_conv = jnp.zeros((bp, 2, CONV_W - 1, 1, D_FF), F32)
    n_phys = cache_k.shape[1]
    cache_k4 = cache_k.reshape(depth, n_phys, PAGE_SIZE * DIFF_HEADS, 2 * DIFF_DH)
    cache_v4 = cache_v.reshape(depth, n_phys, PAGE_SIZE * DIFF_HEADS, DIFF_DV)
    cmem_k = _mem_rows_view(cache_mem_k)
    cmem_v = _mem_rows_view(cache_mem_v)

    outs = [[] for _ in range(10)]
    k_stack = v_stack = None
    for l in range(depth):
        lam_init = 0.8 - 0.6 * math.exp(-0.3 * l)
        wl = w_in[l]
        ga0 = 2 * GLA_QK + 2 * GLA_V
        w_in_l = jnp.concatenate(
            [wl[:, :ga0], wl[:, ga0 + GLA_RANK:], wl[:, ga0:ga0 + GLA_RANK],
             jnp.zeros((D_MODEL, LANES - GLA_RANK), F32)], axis=1).astype(BF16)
        wa2_l = jnp.pad(gla_wa2[l], ((0, LANES - GLA_RANK), (0, 0))).astype(BF16)
        ba_l = gla_ba[l].reshape(1, GLA_QK)
        gnw = gla_norm_w[l].reshape(1, GLA_DV)
        dnw = diff_norm_w[l].reshape(1, DIFF_DV)
        w_out_l = w_out[l].astype(BF16)
        wq_l = cross_wq[l].astype(BF16)
        wk_l = cross_wk[l].astype(BF16)
        wv_l = cross_wv[l].astype(BF16)
        wo_l = cross_wo[l].astype(BF16)
        w_up_l = ffn_w_up[l].astype(BF16)
        w_down_l = ffn_w_down[l].astype(BF16)
        cw_l = ffn_conv_w[l].reshape(CONV_W, 2, D_FF).transpose(1, 0, 2)
        cb_l = ffn_conv_b[l].reshape(2, 1, D_FF)
        r1 = lambda a: a[l].reshape(1, D_MODEL)
        ln = [r1(a) for a in (ln1_g, ln1_b, ln2_g, ln2_b, ln3_g, ln3_b)]

        gq, gk, gg, gv, gr, dq, k_stack, v_stack, dkb, dvb = _inproj(
            xp, w_in_l, wa2_l, ba_l, tabs_p, tp // tm_a, tm_a, stack=(l, depth, k_stack, v_stack))
        b3 = lambda a: a.reshape(bp, tp, a.shape[-1])
        go, s_p = _gla(b3(gq), b3(gk), b3(gg), b3(gv), b3(gr), gnw, None, bp, gla_tt, GLA_CHUNK, GLA_SUB, None)
        do = _diff_prompt(diff_lambda[l], dnw, b3(dq), b3(dkb), b3(dvb), tq, lam_init)
        mk_p, mv_p = _memkv(mem2, wk_l, wv_l)
        x2 = _mix_cross(go, do, xp.reshape(bp, tp, D_MODEL), w_out_l, ln[0], ln[1], wq_l,
                        mk_p.reshape(1, bp, MEM_LEN * _MEM_SPLIT, LANES),
                        mv_p.reshape(1, bp, MEM_LEN * _MEM_SPLIT, LANES), 0,
                        wo_l, ln[2], ln[3], tm_p, alpha)
        x3, conv_p = _ffn(x2, zero_conv, w_up_l, cw_l, cb_l, w_down_l, ln[4], ln[5], tm_p, ffn_tf, 1, tm_p, alpha)
        xp = x3.reshape(bp * tp, D_MODEL)
        outs[4].append(s_p)
        outs[6].append(conv_p[:, -1].transpose(0, 2, 1, 3, 4).reshape(bp, CONV_W - 1, 2 * D_FF))
        outs[8].append(_mem_rows_unview(mk_p.reshape(bp, MEM_LEN * _MEM_SPLIT, LANES)))
        outs[9].append(_mem_rows_unview(mv_p.reshape(bp, MEM_LEN * _MEM_SPLIT, LANES)))

        gq, gk, gg, gv, gr, dq, dkf, dvf, dkb, dvb = _inproj(xs, w_in_l, wa2_l, ba_l, tabs_s, 1, bs * SAMPLE_PAD)
        s3 = lambda a: a.reshape(bs, SAMPLE_PAD, a.shape[-1])
        go, s_s = _gla(s3(gq), s3(gk), s3(gg), s3(gv), s3(gr), gnw, state_gla[l],
                       math.gcd(bs, 8), SAMPLE_PAD, SAMPLE_PAD, SAMPLE_PAD, ts)
        dq_s = s3(dq)[:, :_QROWS].astype(F32)
        do8 = _diff_sample(page_table, diff_lambda[l], dnw, dq_s, s3(dkb), s3(dvb),
                           cache_k4, cache_v4, l, pages, ts, lam_init)
        do = jnp.pad(do8, ((0, 0), (0, SAMPLE_PAD - _QROWS), (0, 0))).astype(BF16)
        x2 = _mix_cross_split(go, do, xs.reshape(bs, SAMPLE_PAD, D_MODEL), w_out_l, ln[0], ln[1], wq_l,
                              cmem_k, cmem_v, l, wo_l, ln[2], ln[3], alpha)
        x2t = x2.transpose(1, 0, 2).reshape(1, SAMPLE_PAD * bs, D_MODEL)
        x3t, conv_s = _ffn(x2t, _conv_state_in(state_conv[l]), w_up_l, cw_l, cb_l, w_down_l, ln[4], ln[5],
                           SAMPLE_PAD * bs, ffn_tf, bs, ts, alpha)
        xs = x3t.reshape(SAMPLE_PAD, bs, D_MODEL).transpose(1, 0, 2).reshape(bs * SAMPLE_PAD, D_MODEL)
        outs[2].append(dkf.reshape(bs, SAMPLE_PAD, DIFF_HEADS, 2 * DIFF_DH)[:, :ts])
        outs[3].append(dvf.reshape(bs, SAMPLE_PAD, DIFF_HEADS, DIFF_DV)[:, :ts])
        outs[5].append(s_s)
        outs[7].append(_conv_state_out(conv_s[:, -1]))

    outs[0] = k_stack.reshape(depth, bp, tp // PAGE_SIZE, PAGE_SIZE, DIFF_HEADS, 2 * DIFF_DH)
    outs[1] = v_stack.reshape(depth, bp, tp // PAGE_SIZE, PAGE_SIZE, DIFF_HEADS, DIFF_DV)
    st = [o if i < 2 else jnp.stack(o) for i, o in enumerate(outs)]
    y_prompt = xp.reshape(bp, tp, D_MODEL)
    y_sample = xs.reshape(bs, SAMPLE_PAD, D_MODEL)[:, :ts]
    return (y_prompt, y_sample, st[0], st[1], st[2], st[3], st[4], st[5], st[6], st[7], st[8], st[9])
```

```python
import functools
import math

import jax
import jax.numpy as jnp
from jax import lax
from jax.experimental import pallas as pl
from jax.experimental.pallas import tpu as pltpu

F32 = jnp.float32
BF16 = jnp.bfloat16

D_MODEL = 1024
PAGE_SIZE = 128
GLA_HEADS = 4
GLA_DK = 64
GLA_DV = 128
GLA_RANK = 16
GLA_TAU = 16.0
GLA_CHUNK = 64
GLA_SUB = 16
DIFF_HEADS = 4
DIFF_DH = 64
DIFF_DV = 128
ROPE_THETA = 10000.0
MEM_LEN = 256
MEM_HEADS = 4
MEM_DH = D_MODEL // MEM_HEADS
D_FF = 2816
CONV_W = 3
LN_EPS = 1e-5
LOG2E = 1.4426950408889634

GLA_QK = GLA_HEADS * GLA_DK
GLA_V = GLA_HEADS * GLA_DV
DIFF_QK = DIFF_HEADS * 2 * DIFF_DH
DIFF_V = DIFF_HEADS * DIFF_DV
LANES = 128
N_IN_PAD = 2 * GLA_QK + 2 * GLA_V + 2 * DIFF_QK + DIFF_V + LANES
SAMPLE_PAD = 16
VMEM_LIMIT = 52 * 1024 * 1024


def _cparams(sem):
    return pltpu.CompilerParams(dimension_semantics=sem, vmem_limit_bytes=VMEM_LIMIT)


def _post_ln(res, sub, g, b, alpha):
    h = alpha * res + sub
    mu = jnp.mean(h, axis=-1, keepdims=True)
    hc = h - mu
    var = jnp.mean(hc * hc, axis=-1, keepdims=True)
    return hc * lax.rsqrt(var + LN_EPS) * g + b


def _dot(a, b):
    return jnp.dot(a, b, preferred_element_type=F32)


def _dot_nt(a, b):
    return lax.dot_general(a, b, (((1,), (1,)), ((), ())), preferred_element_type=F32)


def _dot_tn(a, b):
    return lax.dot_general(a, b, (((0,), (0,)), ((), ())), preferred_element_type=F32)


_O_GQ, _O_GK, _O_GV, _O_GR = 0, GLA_QK, 2 * GLA_QK, 2 * GLA_QK + GLA_V
_O_DQ = 2 * GLA_QK + 2 * GLA_V
_O_DK = _O_DQ + DIFF_QK
_O_DV = _O_DK + DIFF_QK
_O_GA = _O_DV + DIFF_V


def _inproj_kernel(x_ref, w_ref, wa2_ref, ba_ref, c_ref, sa_ref, sb_ref, *rest):
    gq_ref, gk_ref, gg_ref, gv_ref, gr_ref, dq_ref, dkf_ref, dvf_ref, dkb_ref, dvb_ref, dvt_ref = rest[-11:]
    h = _dot(x_ref[...].astype(BF16), w_ref[...])
    gq_ref[...] = h[:, _O_GQ:_O_GQ + GLA_QK] * (GLA_DK ** -0.5)
    gk_ref[...] = h[:, _O_GK:_O_GK + GLA_QK]
    gv_ref[...] = h[:, _O_GV:_O_GV + GLA_V].astype(BF16)
    gr_ref[...] = h[:, _O_GR:_O_GR + GLA_V]
    xg = _dot(h[:, _O_GA:_O_GA + LANES].astype(BF16), wa2_ref[...]) + ba_ref[...]
    gg_ref[...] = (jnp.minimum(xg, 0.0) - jnp.log1p(jnp.exp(-jnp.abs(xg)))) * (1.0 / GLA_TAU)
    c, sa, sb = c_ref[...], sa_ref[...], sb_ref[...]

    def rope(v):
        return v * c + pltpu.roll(v, LANES - DIFF_DH // 2, 1) * sa + pltpu.roll(v, DIFF_DH // 2, 1) * sb

    qscale = (DIFF_DH ** -0.5) * LOG2E
    tm = h.shape[0]
    for j in range(DIFF_HEADS):
        sl = slice(j * LANES, (j + 1) * LANES)
        head_rows = pl.ds(j, tm, stride=DIFF_HEADS)
        dq_ref[:, sl] = (rope(h[:, _O_DQ + j * LANES:_O_DQ + (j + 1) * LANES]) * qscale).astype(BF16)
        rk = rope(h[:, _O_DK + j * LANES:_O_DK + (j + 1) * LANES])
        dkf_ref[head_rows, :] = rk
        dkb_ref[:, sl] = rk.astype(BF16)
        dv = h[:, _O_DV + j * LANES:_O_DV + (j + 1) * LANES]
        dvf_ref[head_rows, :] = dv
        dvb_ref[:, sl] = dv.astype(BF16)
        dvt_ref[j] = dv.T.astype(BF16)


_I_DKF, _I_DVF = 6, 7


def _inproj(x, w, wa2, ba, tabs, tab_blocks, tm, stack=None):
    m = x.shape[0]
    row = lambda n: pl.BlockSpec((tm, n), lambda i: (i, 0))
    full = lambda a: pl.BlockSpec(a.shape, lambda i: (0,) * a.ndim)
    tab = pl.BlockSpec((tm, LANES), lambda i: (i % tab_blocks, 0))
    shapes = [(1, GLA_QK, F32), (1, GLA_QK, F32), (1, GLA_QK, F32), (1, GLA_V, BF16), (1, GLA_V, F32),
              (1, DIFF_QK, BF16), (DIFF_HEADS, LANES, F32), (DIFF_HEADS, LANES, F32),
              (1, DIFF_QK, BF16), (1, DIFF_V, BF16)]
    out_specs = [pl.BlockSpec((tm * k, n), lambda i: (i, 0)) for k, n, _ in shapes]
    out_shape = [jax.ShapeDtypeStruct((m * k, n), dt) for k, n, dt in shapes]
    out_specs.append(pl.BlockSpec((None, DIFF_HEADS, DIFF_DV, tm), lambda i: (i, 0, 0, 0)))
    out_shape.append(jax.ShapeDtypeStruct((m // tm, DIFF_HEADS, DIFF_DV, tm), BF16))
    in_specs = [row(D_MODEL), full(w), full(wa2), full(ba), tab, tab, tab]
    args = [x, w, wa2, ba, *tabs]
    aliases = {}
    if stack is not None:
        layer, depth, kbuf, vbuf = stack
        for idx, buf in ((_I_DKF, kbuf), (_I_DVF, vbuf)):
            k, n, dt = shapes[idx]
            out_specs[idx] = pl.BlockSpec((None, tm * k, n), lambda i: (layer, i, 0))
            out_shape[idx] = jax.ShapeDtypeStruct((depth, m * k, n), dt)
            if buf is not None:
                aliases[len(args)] = idx
                in_specs.append(pl.BlockSpec(memory_space=pl.ANY))
                args.append(buf)
    return pl.pallas_call(
        _inproj_kernel,
        grid=(m // tm,),
        in_specs=in_specs,
        out_specs=out_specs,
        out_shape=out_shape,
        input_output_aliases=aliases,
        compiler_params=_cparams(("parallel",)),
        name="inproj",
    )(*args)


def _gla_chunk(qc, kc, gc, vc, st, tril, head_a, st_mask, chunk, sub):
    nsub = chunk // sub
    g1 = gc.astype(BF16)
    r1 = gc - g1.astype(F32)
    g2 = r1.astype(BF16)
    g3 = (r1 - g2.astype(F32)).astype(BF16)
    b3 = _dot(tril, jnp.concatenate([g1, g2, g3], axis=1))
    b = b3[:, 0:LANES] + b3[:, LANES:2 * LANES] + b3[:, 2 * LANES:]
    b_last = b[chunk - 1:chunk, :]

    def by_head(x):
        return jnp.concatenate([jnp.where(head_a, x, 0.0), jnp.where(head_a, 0.0, x)], axis=0).astype(BF16)

    oi = _dot_nt(by_head(qc * jnp.exp(b)), st.astype(BF16))
    o_inter = jnp.concatenate([oi[0:chunk, 0:GLA_DV], oi[chunk:, GLA_DV:]], axis=1)
    key_row = lax.broadcasted_iota(jnp.int32, (chunk, 1), 0)
    lqs, kss = [], []
    for s in range(nsub):
        r0, r1_ = s * sub, (s + 1) * sub
        anchor = b[r0:r0 + 1, :]
        lqs.append(by_head(qc[r0:r1_] * jnp.exp(b[r0:r1_] - anchor)))
        kss.append((kc * jnp.exp(jnp.where(key_row < r1_, anchor - b, 0.0))).astype(BF16))
    att = lax.dot_general(jnp.stack(lqs), jnp.stack(kss), (((2,), (2,)), ((0,), (0,))),
                          preferred_element_type=F32)
    q_tok = (lax.broadcasted_iota(jnp.int32, att.shape, 0) * sub
             + lax.broadcasted_iota(jnp.int32, att.shape, 1) % sub)
    att = jnp.where(lax.broadcasted_iota(jnp.int32, att.shape, 2) <= q_tok, att, 0.0)
    ov = _dot(att.reshape(nsub * 2 * sub, chunk).astype(BF16), vc)
    o_intra = jnp.concatenate(
        [jnp.concatenate([ov[2 * s * sub:(2 * s + 1) * sub, 0:GLA_DV],
                          ov[(2 * s + 1) * sub:(2 * s + 2) * sub, GLA_DV:]], axis=1) for s in range(nsub)], axis=0)
    upd = _dot_tn(vc, (kc * jnp.exp(b_last - b)).astype(BF16))
    return o_inter + o_intra, st * jnp.exp(b_last) + jnp.where(st_mask, upd, 0.0)


def _gla_kernel(*refs, bb, tt, chunk, sub, t_valid, has_s0):
    if has_s0:
        q_ref, k_ref, g_ref, v_ref, gr_ref, nw_ref, s0_ref, o_ref, s_out_ref, st_scr = refs
    else:
        q_ref, k_ref, g_ref, v_ref, gr_ref, nw_ref, o_ref, s_out_ref, st_scr = refs
    i = pl.program_id(1)
    pairs = GLA_HEADS // 2
    head_a = lax.broadcasted_iota(jnp.int32, (1, LANES), 1) < GLA_DK
    st_mask = (lax.broadcasted_iota(jnp.int32, (2 * GLA_DV, LANES), 0) // GLA_DV
               == lax.broadcasted_iota(jnp.int32, (2 * GLA_DV, LANES), 1) // GLA_DK)
    tril = (lax.broadcasted_iota(jnp.int32, (chunk, chunk), 0)
            >= lax.broadcasted_iota(jnp.int32, (chunk, chunk), 1)).astype(BF16)
    row_in_chunk = lax.broadcasted_iota(jnp.int32, (chunk, 1), 0)

    @pl.when(i == 0)
    def _():
        for bi in range(bb):
            for p in range(pairs):
                if has_s0:
                    s0t = s0_ref[bi, 2 * p:2 * p + 2].reshape(2 * GLA_DK, GLA_DV).T
                    st_scr[bi, p] = jnp.where(st_mask, jnp.concatenate([s0t, s0t], axis=0), 0.0)
                else:
                    st_scr[bi, p] = jnp.zeros((2 * GLA_DV, LANES), F32)

    nw = nw_ref[...]
    for bi in range(bb):
        for p in range(pairs):
            ks = slice(p * LANES, (p + 1) * LANES)
            vs = slice(p * 2 * GLA_DV, (p + 1) * 2 * GLA_DV)
            st = st_scr[bi, p]
            for c in range(tt // chunk):
                rs = slice(c * chunk, (c + 1) * chunk)
                qc, kc, gc, vc = q_ref[bi, rs, ks], k_ref[bi, rs, ks], g_ref[bi, rs, ks], v_ref[bi, rs, vs]
                if t_valid is not None:
                    ok = (row_in_chunk + c * chunk) < t_valid
                    kc = jnp.where(ok, kc, 0.0)
                    gc = jnp.where(ok, gc, 0.0)
                    vc = jnp.where(ok, vc, jnp.zeros_like(vc))
                o, st = _gla_chunk(qc, kc, gc, vc, st, tril, head_a, st_mask, chunk, sub)
                gr = gr_ref[bi, rs, vs]
                normed = []
                for hh in range(2):
                    oh = o[:, hh * GLA_DV:(hh + 1) * GLA_DV]
                    normed.append(oh * lax.rsqrt(jnp.mean(oh * oh, axis=-1, keepdims=True) + LN_EPS) * nw)
                gate = gr * (1.0 / (1.0 + jnp.exp(-gr)))
                o_ref[bi, rs, vs] = (jnp.concatenate(normed, axis=1) * gate).astype(o_ref.dtype)
            st_scr[bi, p] = st

    @pl.when(i == pl.num_programs(1) - 1)
    def _():
        for bi in range(bb):
            for p in range(pairs):
                st = st_scr[bi, p]
                sa = st[0:GLA_DV].T
                sb = st[GLA_DV:].T
                s_out_ref[bi, 2 * p:2 * p + 2] = jnp.concatenate(
                    [sa[0:GLA_DK], sb[GLA_DK:]], axis=0).reshape(2, GLA_DK, GLA_DV)


def _gla(q, k, g, v, gr, nw, s0, bb, tt, chunk, sub, t_valid):
    bsz, t, _ = q.shape
    has_s0 = s0 is not None
    qk_spec = pl.BlockSpec((bb, tt, GLA_QK), lambda b, i: (b, i, 0))
    v_spec = pl.BlockSpec((bb, tt, GLA_V), lambda b, i: (b, i, 0))
    s_spec = pl.BlockSpec((bb, GLA_HEADS, GLA_DK, GLA_DV), lambda b, i: (b, 0, 0, 0))
    in_specs = [qk_spec, qk_spec, qk_spec, v_spec, v_spec, pl.BlockSpec(nw.shape, lambda b, i: (0, 0))]
    args = [q, k, g, v, gr, nw]
    if has_s0:
        in_specs.append(s_spec)
        args.append(s0)
    return pl.pallas_call(
        functools.partial(_gla_kernel, bb=bb, tt=tt, chunk=chunk, sub=sub, t_valid=t_valid, has_s0=has_s0),
        grid=(bsz // bb, t // tt),
        in_specs=in_specs,
        out_specs=[v_spec, s_spec],
        out_shape=[jax.ShapeDtypeStruct((bsz, t, GLA_V), BF16),
                   jax.ShapeDtypeStruct((bsz, GLA_HEADS, GLA_DK, GLA_DV), F32)],
        scratch_shapes=[pltpu.VMEM((bb, GLA_HEADS // 2, 2 * GLA_DV, LANES), F32)],
        compiler_params=_cparams(("parallel", "arbitrary")),
        name="gla",
    )(*args)


def _diff_lambda(lam_ref, lam_init):
    lf = lam_ref[...]
    a = jnp.sum(lf[0:1] * lf[1:2], axis=-1, keepdims=True)
    b = jnp.sum(lf[2:3] * lf[3:4], axis=-1, keepdims=True)
    return jnp.exp(a) - jnp.exp(b) + lam_init


def _diff_finish(o0, o1, lam, nw, lam_init):
    od = o0 - lam * o1
    ms = jnp.mean(od * od, axis=-1, keepdims=True)
    return od * lax.rsqrt(ms + LN_EPS) * nw * (1.0 - lam_init)


_ONES_ROWS = 16


def _diff_prompt_kernel(lam_ref, nw_ref, q_ref, k_ref, vt_ref, o_ref,
                        qz_scr, s_scr, p_scr, m_scr, a_scr, acc_scr, *, tq, lam_init):
    qi = pl.program_id(2)
    q = q_ref[...]
    lane = lax.broadcasted_iota(jnp.int32, (1, LANES), 1)
    zero = jnp.zeros_like(q)
    qz_scr[0:tq, :] = jnp.where(lane < DIFF_DH, q, zero)
    qz_scr[tq:, :] = jnp.where(lane < DIFF_DH, zero, q)
    m_scr[...] = jnp.full_like(m_scr, -jnp.inf)
    acc_scr[...] = jnp.zeros_like(acc_scr)
    p_scr[1] = jnp.zeros(p_scr.shape[1:], BF16)
    a_scr[1] = jnp.ones(a_scr.shape[1:], F32)
    ones = jnp.ones((_ONES_ROWS, tq), BF16)
    key_row = lax.broadcasted_iota(jnp.int32, (tq, 1), 0)

    def scores(t, slot):
        ks = pl.multiple_of(t * tq, tq)
        s_scr[slot] = _dot_nt(k_ref[pl.ds(ks, tq), :], qz_scr[...])

    def softmax(slot, masked):
        for c in range(2 * tq // LANES):
            cols = slice(c * LANES, (c + 1) * LANES)
            s = s_scr[slot, :, cols]
            if masked:
                s = jnp.where(key_row <= lane + (c * LANES) % tq, s, -jnp.inf)
            m_prev = m_scr[:, cols]
            m_new = jnp.maximum(m_prev, jnp.max(s, axis=0, keepdims=True))
            a_scr[slot, :, cols] = jnp.exp2(m_prev - m_new)
            m_scr[:, cols] = m_new
            p_scr[slot, :, cols] = jnp.exp2(s - m_new[0:1]).astype(BF16)

    def values(t, slot):
        v1t = jnp.concatenate([vt_ref[t], ones], axis=0)
        acc_scr[...] = acc_scr[...] * a_scr[slot, 0:1, :] + _dot(v1t, p_scr[slot])

    scores(0, 0)

    def step(t, slot):
        scores(t + 1, 1 - slot)
        softmax(slot, False)
        values(jnp.maximum(t - 1, 0), 1 - slot)

    def body(u, carry):
        step(2 * u, 0)
        step(2 * u + 1, 1)
        return carry

    lax.fori_loop(0, qi // 2, body, 0)

    def tail(slot):
        softmax(slot, True)
        values(jnp.maximum(qi - 1, 0), 1 - slot)
        values(qi, slot)

    @pl.when(qi % 2 == 0)
    def _():
        tail(0)

    @pl.when(qi % 2 == 1)
    def _():
        softmax(0, False)
        values(jnp.maximum(qi - 2, 0), 1)
        values(qi - 1, 0)
        scores(qi, 1)
        softmax(1, True)
        values(qi, 1)

    acc = acc_scr[...]
    o = acc[0:DIFF_DV, :] / acc[DIFF_DV:DIFF_DV + 1, :]
    lam = _diff_lambda(lam_ref, lam_init)
    o_ref[...] = _diff_finish(o[:, 0:tq].T, o[:, tq:].T, lam, nw_ref[...], lam_init).astype(o_ref.dtype)


def _diff_prompt(lam_p, nw, q, k, vt, tq, lam_init):
    bsz, t, _ = q.shape
    qspec = pl.BlockSpec((None, tq, LANES), lambda b, h, i: (b, i, h))
    kspec = pl.BlockSpec((None, t, LANES), lambda b, h, i: (b, 0, h))
    vspec = pl.BlockSpec((None, t // tq, None, DIFF_DV, tq), lambda b, h, i: (b, 0, h, 0, 0))
    small = lambda a: pl.BlockSpec(a.shape, lambda b, h, i: (0,) * a.ndim)
    return pl.pallas_call(
        functools.partial(_diff_prompt_kernel, tq=tq, lam_init=lam_init),
        grid=(bsz, DIFF_HEADS, t // tq),
        in_specs=[small(lam_p), small(nw), qspec, kspec, vspec],
        out_specs=qspec,
        out_shape=jax.ShapeDtypeStruct((bsz, t, DIFF_V), BF16),
        scratch_shapes=[pltpu.VMEM((2 * tq, LANES), BF16),
                        pltpu.VMEM((2, tq, 2 * tq), F32),
                        pltpu.VMEM((2, tq, 2 * tq), BF16),
                        pltpu.VMEM((8, 2 * tq), F32),
                        pltpu.VMEM((2, 8, 2 * tq), F32),
                        pltpu.VMEM((DIFF_DV + _ONES_ROWS, 2 * tq), F32)],
        compiler_params=_cparams(("parallel", "parallel", "arbitrary")),
        name="diff_prompt",
    )(lam_p, nw, q, k, vt)


_QROWS = 8


def _diff_sample_kernel(pt_ref, lam_ref, nw_ref, q_ref, kn_ref, vn_ref, *rest, pages, t_new, lam_init):
    k_refs, v_refs = rest[:pages], rest[pages:2 * pages]
    o_ref, qz_scr, m_scr, l_scr, acc_scr = rest[2 * pages:]
    j = pl.program_id(1)
    grp = 2 * _QROWS

    def update(keys, vals, mask):
        s = jnp.concatenate([_dot_nt(qz_scr[h * grp:(h + 1) * grp, :], keys[h]) for h in range(DIFF_HEADS)], axis=0)
        if mask is not None:
            s = jnp.where(mask, s, -jnp.inf)
        m_prev = m_scr[:, 0:1]
        m_new = jnp.maximum(m_prev, jnp.max(s, axis=1, keepdims=True))
        alpha = jnp.exp2(m_prev - m_new)
        p = jnp.exp2(s - m_new)
        l_new = l_scr[:, 0:1] * alpha + jnp.sum(p, axis=1, keepdims=True)
        pb = p.astype(BF16)
        pv = jnp.concatenate([_dot(pb[h * grp:(h + 1) * grp], vals[h]) for h in range(DIFF_HEADS)], axis=0)
        l_scr[...] = jnp.broadcast_to(l_new, l_scr.shape)
        acc_scr[...] = acc_scr[...] * alpha + pv
        m_scr[...] = jnp.broadcast_to(m_new, m_scr.shape)

    @pl.when(j == 0)
    def _():
        q = q_ref[...]
        lane = lax.broadcasted_iota(jnp.int32, (1, LANES), 1)
        m_scr[...] = jnp.full_like(m_scr, -jnp.inf)
        l_scr[...] = jnp.zeros_like(l_scr)
        acc_scr[...] = jnp.zeros_like(acc_scr)
        for h in range(DIFF_HEADS):
            qh = q[:, h * LANES:(h + 1) * LANES]
            qz_scr[h * grp:(h + 1) * grp, :] = jnp.concatenate(
                [jnp.where(lane < DIFF_DH, qh, 0.0), jnp.where(lane < DIFF_DH, 0.0, qh)], axis=0).astype(BF16)
        rr = lax.broadcasted_iota(jnp.int32, (DIFF_HEADS * grp, SAMPLE_PAD), 0) % _QROWS
        cc = lax.broadcasted_iota(jnp.int32, (DIFF_HEADS * grp, SAMPLE_PAD), 1)
        update([kn_ref[:, h * LANES:(h + 1) * LANES] for h in range(DIFF_HEADS)],
               [vn_ref[:, h * LANES:(h + 1) * LANES] for h in range(DIFF_HEADS)], (cc <= rr) & (cc < t_new))

    def head_of(refs, h):
        return jnp.concatenate([r[pl.ds(h, PAGE_SIZE, stride=DIFF_HEADS), :].astype(BF16) for r in refs], axis=0)

    update([head_of(k_refs, h) for h in range(DIFF_HEADS)], [head_of(v_refs, h) for h in range(DIFF_HEADS)], None)

    @pl.when(j == pl.num_programs(1) - 1)
    def _():
        acc = acc_scr[...]
        inv = 1.0 / l_scr[:, 0:1]
        lam = _diff_lambda(lam_ref, lam_init)
        nw = nw_ref[...]
        outs = []
        for h in range(DIFF_HEADS):
            r0 = h * grp
            r1 = r0 + _QROWS
            o0 = acc[r0:r0 + _QROWS] * inv[r0:r0 + _QROWS]
            o1 = acc[r1:r1 + _QROWS] * inv[r1:r1 + _QROWS]
            outs.append(_diff_finish(o0, o1, lam, nw, lam_init))
        o_ref[...] = jnp.concatenate(outs, axis=1)


def _diff_sample(page_table, lam_p, nw, q, k_new, v_new, cache_k, cache_v, layer, pages, t_new, lam_init):
    bsz, n_pages = page_table.shape
    small = lambda a: pl.BlockSpec(a.shape, lambda b, j, pt: (0,) * a.ndim)
    qspec = pl.BlockSpec((None, _QROWS, DIFF_QK), lambda b, j, pt: (b, 0, 0))
    nspec = pl.BlockSpec((None, SAMPLE_PAD, DIFF_QK), lambda b, j, pt: (b, 0, 0))

    def page_spec(i):
        return pl.BlockSpec((None, None, PAGE_SIZE * DIFF_HEADS, LANES),
                            lambda b, j, pt: (layer, pt[b, j * pages + i], 0, 0))

    nrow = 2 * DIFF_HEADS * _QROWS
    grid_spec = pltpu.PrefetchScalarGridSpec(
        num_scalar_prefetch=1,
        grid=(bsz, n_pages // pages),
        in_specs=[small(lam_p), small(nw), qspec, nspec, nspec]
                 + [page_spec(i) for i in range(pages)] + [page_spec(i) for i in range(pages)],
        out_specs=qspec,
        scratch_shapes=[pltpu.VMEM((nrow, LANES), BF16), pltpu.VMEM((nrow, LANES), F32),
                        pltpu.VMEM((nrow, LANES), F32), pltpu.VMEM((nrow, DIFF_DV), F32)],
    )
    return pl.pallas_call(
        functools.partial(_diff_sample_kernel, pages=pages, t_new=t_new, lam_init=lam_init),
        grid_spec=grid_spec,
        out_shape=jax.ShapeDtypeStruct((bsz, _QROWS, DIFF_V), F32),
        compiler_params=_cparams(("parallel", "arbitrary")),
        name="diff_sample",
    )(page_table, lam_p, nw, q, k_new, v_new, *([cache_k] * pages), *([cache_v] * pages))


_MEM_SPLIT = D_MODEL // LANES
_MEM_PIECES = MEM_DH // LANES


def _mem_row(head, piece):
    return piece * MEM_HEADS + head


def _mem_rows_view(a):
    lead = a.shape[:-3]
    n = len(lead)
    a = a.reshape(*lead, MEM_LEN, MEM_HEADS, _MEM_PIECES, LANES)
    a = a.transpose(*range(n), n, n + 2, n + 1, n + 3)
    return a.reshape(*lead, MEM_LEN * _MEM_SPLIT, LANES)


def _mem_rows_unview(a):
    lead = a.shape[:-2]
    n = len(lead)
    a = a.reshape(*lead, MEM_LEN, _MEM_PIECES, MEM_HEADS, LANES)
    a = a.transpose(*range(n), n, n + 2, n + 1, n + 3)
    return a.reshape(*lead, MEM_LEN, MEM_HEADS, MEM_DH)


def _memkv_kernel(x_ref, wk_ref, wv_ref, k_ref, v_ref):
    x = x_ref[...].astype(BF16)
    m = x.shape[0]
    for w_ref, o_ref in ((wk_ref, k_ref), (wv_ref, v_ref)):
        y = _dot(x, w_ref[...])
        for j in range(_MEM_SPLIT):
            o_ref[pl.ds(_mem_row(j // _MEM_PIECES, j % _MEM_PIECES), m, stride=_MEM_SPLIT), :] = (
                y[:, j * LANES:(j + 1) * LANES])


def _memkv(mem, wk, wv):
    m = mem.shape[0]
    full = lambda a: pl.BlockSpec(a.shape, lambda i: (0, 0))
    out = pl.BlockSpec((m * _MEM_SPLIT, LANES), lambda i: (0, 0))
    return pl.pallas_call(
        _memkv_kernel,
        grid=(1,),
        in_specs=[full(mem), full(wk), full(wv)],
        out_specs=[out, out],
        out_shape=[jax.ShapeDtypeStruct((m * _MEM_SPLIT, LANES), F32)] * 2,
        compiler_params=_cparams(("arbitrary",)),
        name="memkv",
    )(mem, wk, wv)


def _mix_proj(g_ref, d_ref, x_ref, wo1_ref, ln1g_ref, ln1b_ref, wq_ref, alpha):
    mix = _dot(g_ref[...], wo1_ref[0:GLA_V, :]) + _dot(d_ref[...], wo1_ref[GLA_V:, :])
    x1 = _post_ln(x_ref[...], mix, ln1g_ref[...], ln1b_ref[...], alpha)
    return x1, (_dot(x1.astype(BF16), wq_ref[...]) * ((MEM_DH ** -0.5) * LOG2E)).astype(BF16)


def _cross_heads(q, mk_ref, mv_ref):
    heads = []
    for h in range(MEM_HEADS):
        def head_of(ref):
            return jnp.concatenate([ref[pl.ds(_mem_row(h, j), MEM_LEN, stride=_MEM_SPLIT), :]
                                    for j in range(_MEM_PIECES)], axis=1).astype(BF16)

        s = _dot_nt(q[:, h * MEM_DH:(h + 1) * MEM_DH], head_of(mk_ref))
        p = jnp.exp2(s - jnp.max(s, axis=-1, keepdims=True))
        l = jnp.sum(p, axis=-1, keepdims=True)
        heads.append((_dot(p.astype(BF16), head_of(mv_ref)) / l).astype(BF16))
    return jnp.concatenate(heads, axis=1)


def _mix_cross_kernel(g_ref, d_ref, x_ref, wo1_ref, ln1g_ref, ln1b_ref, wq_ref, mk_ref, mv_ref, wo2_ref,
                      ln2g_ref, ln2b_ref, o_ref, *, alpha):
    x1, q = _mix_proj(g_ref, d_ref, x_ref, wo1_ref, ln1g_ref, ln1b_ref, wq_ref, alpha)
    y = _dot(_cross_heads(q, mk_ref, mv_ref), wo2_ref[...])
    o_ref[...] = _post_ln(x1, y, ln2g_ref[...], ln2b_ref[...], alpha)


def _mix_proj_kernel(g_ref, d_ref, x_ref, wo1_ref, ln1g_ref, ln1b_ref, wq_ref, x1_ref, q_ref, *, alpha):
    x1_ref[...], q_ref[...] = _mix_proj(g_ref, d_ref, x_ref, wo1_ref, ln1g_ref, ln1b_ref, wq_ref, alpha)


def _cross_kernel(q_ref, mk_ref, mv_ref, o_ref):
    o_ref[...] = _cross_heads(q_ref[...], mk_ref, mv_ref)


def _out_proj_kernel(o_ref, x1_ref, wo2_ref, ln2g_ref, ln2b_ref, y_ref, *, alpha):
    y_ref[...] = _post_ln(x1_ref[...], _dot(o_ref[...], wo2_ref[...]), ln2g_ref[...], ln2b_ref[...], alpha)


def _mix_cross_split(g, d, x, w_out, ln1g, ln1b, wq, mk, mv, mem_layer, wo, ln2g, ln2b, alpha):
    bsz, r, _ = x.shape
    rows = bsz * r
    flat = lambda a: a.reshape(rows, a.shape[-1])
    full = lambda a: pl.BlockSpec(a.shape, lambda *_: (0,) * a.ndim)
    x1, q = pl.pallas_call(
        functools.partial(_mix_proj_kernel, alpha=alpha),
        grid=(1,),
        in_specs=[full(flat(g)), full(flat(d)), full(flat(x)), full(w_out), full(ln1g), full(ln1b), full(wq)],
        out_specs=[full(flat(x)), full(flat(x))],
        out_shape=[jax.ShapeDtypeStruct((rows, D_MODEL), F32), jax.ShapeDtypeStruct((rows, D_MODEL), BF16)],
        compiler_params=_cparams(("arbitrary",)),
        name="mix_proj",
    )(flat(g), flat(d), flat(x), w_out, ln1g, ln1b, wq)
    seq = pl.BlockSpec((None, r, D_MODEL), lambda b: (b, 0, 0))
    mem = pl.BlockSpec((None, None, MEM_LEN * _MEM_SPLIT, LANES), lambda b: (mem_layer, b, 0, 0))
    o = pl.pallas_call(
        _cross_kernel,
        grid=(bsz,),
        in_specs=[seq, mem, mem],
        out_specs=seq,
        out_shape=jax.ShapeDtypeStruct((bsz, r, D_MODEL), BF16),
        compiler_params=_cparams(("parallel",)),
        name="cross",
    )(q.reshape(bsz, r, D_MODEL), mk, mv)
    y = pl.pallas_call(
        functools.partial(_out_proj_kernel, alpha=alpha),
        grid=(1,),
        in_specs=[full(flat(x)), full(flat(x)), full(wo), full(ln2g), full(ln2b)],
        out_specs=full(flat(x)),
        out_shape=jax.ShapeDtypeStruct((rows, D_MODEL), F32),
        compiler_params=_cparams(("arbitrary",)),
        name="out_proj",
    )(flat(o), x1, wo, ln2g, ln2b)
    return y.reshape(bsz, r, D_MODEL)


def _mix_cross(g, d, x, w_out, ln1g, ln1b, wq, mk, mv, mem_layer, wo, ln2g, ln2b, tm, alpha):
    bsz, r, _ = x.shape
    row = lambda n: pl.BlockSpec((None, tm, n), lambda b, i: (b, i, 0))
    full = lambda a: pl.BlockSpec(a.shape, lambda b, i: (0,) * a.ndim)
    mem = pl.BlockSpec((None, None, MEM_LEN * _MEM_SPLIT, LANES), lambda b, i: (mem_layer, b, 0, 0))
    return pl.pallas_call(
        functools.partial(_mix_cross_kernel, alpha=alpha),
        grid=(bsz, r // tm),
        in_specs=[row(GLA_V), row(DIFF_V), row(D_MODEL), full(w_out), full(ln1g), full(ln1b), full(wq),
                  mem, mem, full(wo), full(ln2g), full(ln2b)],
        out_specs=row(D_MODEL),
        out_shape=jax.ShapeDtypeStruct(x.shape, F32),
        compiler_params=_cparams(("parallel", "parallel")),
        name="mix_cross",
    )(g, d, x, w_out, ln1g, ln1b, wq, mk, mv, wo, ln2g, ln2b)


_CARRY = 8
_FF_CHUNK = 256


def _ffn_kernel(x_ref, st_ref, wua_ref, wub_ref, cw_ref, cb_ref, wd_ref, g_ref, b_ref,
                o_ref, ns_ref, carry_scr, h_scr, *, tm, stride, n_valid, alpha):
    i = pl.program_id(1)
    x = x_ref[...]
    xb = x.astype(BF16)
    rows8 = lax.broadcasted_iota(jnp.int32, (_CARRY, 1), 0)

    if stride == 1:
        @pl.when(i == 0)
        def _():
            for half in range(2):
                carry_scr[half, _CARRY - 2:_CARRY - 1, :] = st_ref[half, 0]
                carry_scr[half, _CARRY - 1:_CARRY, :] = st_ref[half, 1]

    for c in range(D_FF // _FF_CHUNK):
        cs = slice(c * _FF_CHUNK, (c + 1) * _FF_CHUNK)
        halves = []
        for half, w_ref in enumerate((wua_ref, wub_ref)):
            u = _dot(xb, w_ref[:, cs])
            if stride == 1:
                p1 = carry_scr[half, _CARRY - 1:_CARRY, cs]
                p2 = carry_scr[half, _CARRY - 2:_CARRY - 1, cs]
                r1, r2 = pltpu.roll(u, 1, 0), pltpu.roll(u, 2, 0)
                u1 = jnp.concatenate([jnp.where(rows8 == 0, p1, r1[0:_CARRY]), r1[_CARRY:]], axis=0)
                u2 = jnp.concatenate(
                    [jnp.where(rows8 == 0, p2, jnp.where(rows8 == 1, p1, r2[0:_CARRY])), r2[_CARRY:]], axis=0)
                carry_scr[half, :, cs] = u[tm - _CARRY:tm, :]
            else:
                padded = jnp.concatenate([st_ref[half, 0, :, cs], st_ref[half, 1, :, cs], u], axis=0)
                u1 = padded[stride:stride + tm]
                u2 = padded[0:tm]
            cw = cw_ref[half, :, cs]
            halves.append(cb_ref[half, :, cs] + cw[0:1] * u2 + cw[1:2] * u1 + cw[2:3] * u)
            for j in range(CONV_W - 1):
                r0 = (n_valid - (CONV_W - 1) + j) * stride
                ns_ref[half, j, :, cs] = u[r0:r0 + stride, :]
        a, bb = halves
        h_scr[:, cs] = (a * (1.0 / (1.0 + jnp.exp(-a))) * bb).astype(BF16)
    o_ref[...] = _post_ln(x, _dot(h_scr[...], wd_ref[...]), g_ref[...], b_ref[...], alpha)


def _ffn(x, state, w_up, conv_w, conv_b, w_down, g, b, tm, stride, n_valid, alpha):
    bsz, r, _ = x.shape
    assert stride == 1 or r == tm
    row = pl.BlockSpec((None, tm, D_MODEL), lambda bi, i: (bi, i, 0))
    st = pl.BlockSpec((None, 2, CONV_W - 1, stride, D_FF), lambda bi, i: (bi, 0, 0, 0, 0))
    once = pl.Buffered(1)
    full = lambda a: pl.BlockSpec(a.shape, lambda bi, i: (0,) * a.ndim, pipeline_mode=once)
    return pl.pallas_call(
        functools.partial(_ffn_kernel, tm=tm, stride=stride, n_valid=n_valid, alpha=alpha),
        grid=(bsz, r // tm),
        in_specs=[row, st,
                  pl.BlockSpec((D_MODEL, D_FF), lambda bi, i: (0, 0), pipeline_mode=once),
                  pl.BlockSpec((D_MODEL, D_FF), lambda bi, i: (0, 1), pipeline_mode=once),
                  full(conv_w), full(conv_b), full(w_down), full(g), full(b)],
        out_specs=[row, pl.BlockSpec((None, None, 2, CONV_W - 1, stride, D_FF), lambda bi, i: (bi, i, 0, 0, 0, 0))],
        out_shape=[jax.ShapeDtypeStruct(x.shape, F32),
                   jax.ShapeDtypeStruct((bsz, r // tm, 2, CONV_W - 1, stride, D_FF), F32)],
        scratch_shapes=[pltpu.VMEM((2, _CARRY, D_FF), F32), pltpu.VMEM((tm, D_FF), BF16)],
        compiler_params=_cparams(("parallel", "arbitrary")),
        name="ffn",
    )(x, state, w_up, w_up, conv_w, conv_b, w_down, g, b)


def _rope_tables(pos):
    half = DIFF_DH // 2
    inv = ROPE_THETA ** (-jnp.arange(half, dtype=F32) / half)
    ang = pos.astype(F32)[:, None] * inv[None, :]
    cos = jnp.tile(jnp.cos(ang), (1, LANES // half))
    sin = jnp.tile(jnp.sin(ang), (1, LANES // half))
    first = (jnp.arange(LANES) % DIFF_DH) < half
    return cos, jnp.where(first, -sin, 0.0), jnp.where(first, 0.0, sin)


def _conv_state_in(s):
    return s.reshape(s.shape[0], CONV_W - 1, 2, D_FF).transpose(2, 1, 0, 3)[None]


def _conv_state_out(s):
    return s[0].transpose(2, 1, 0, 3).reshape(s.shape[3], CONV_W - 1, 2 * D_FF)


def kernel(x_prompt, x_sample, mem_prompt, cache_k, cache_v, page_table, cache_mem_k, cache_mem_v,
           state_gla, state_conv, w_in, gla_wa2, gla_ba, gla_norm_w, diff_lambda, diff_norm_w, w_out,
           ln1_g, ln1_b, cross_wq, cross_wk, cross_wv, cross_wo, ln2_g, ln2_b,
           ffn_w_up, ffn_conv_w, ffn_conv_b, ffn_w_down, ln3_g, ln3_b):
    depth = w_in.shape[0]
    bp, tp, _ = x_prompt.shape
    bs, ts, _ = x_sample.shape
    n_pages = page_table.shape[1]
    past_len = n_pages * PAGE_SIZE
    alpha = (2 * depth) ** 0.25
    assert ts <= _QROWS and tp % 512 == 0

    tm_p = 512
    tq = 512
    gla_tt = 256
    pages = math.gcd(n_pages, 16)

    tabs_p = _rope_tables(jnp.arange(tp))
    pos_s = past_len + jnp.minimum(jnp.arange(SAMPLE_PAD), ts - 1)
    tabs_s = _rope_tables(jnp.tile(pos_s, bs))

    xp = x_prompt.reshape(bp * tp, D_MODEL)
    xs = jnp.pad(x_sample, ((0, 0), (0, SAMPLE_PAD - ts), (0, 0))).reshape(bs * SAMPLE_PAD, D_MODEL)
    mem2 = mem_prompt.reshape(bp * MEM_LEN, D_MODEL)
    zero_conv = jnp.zeros((bp, 2, CONV_W - 1, 1, D_FF), F32)
    n_phys = cache_k.shape[1]
    cache_k4 = cache_k.reshape(depth, n_phys, PAGE_SIZE * DIFF_HEADS, 2 * DIFF_DH)
    cache_v4 = cache_v.reshape(depth, n_phys, PAGE_SIZE * DIFF_HEADS, DIFF_DV)
    cmem_k = _mem_rows_view(cache_mem_k)
    cmem_v = _mem_rows_view(cache_mem_v)

    outs = [[] for _ in range(10)]
    k_stack = v_stack = None
    for l in range(depth):
        lam_init = 0.8 - 0.6 * math.exp(-0.3 * l)
        wl = w_in[l]
        ga0 = 2 * GLA_QK + 2 * GLA_V
        w_in_l = jnp.concatenate(
            [wl[:, :ga0], wl[:, ga0 + GLA_RANK:], wl[:, ga0:ga0 + GLA_RANK],
             jnp.zeros((D_MODEL, LANES - GLA_RANK), F32)], axis=1).astype(BF16)
        wa2_l = jnp.pad(gla_wa2[l], ((0, LANES - GLA_RANK), (0, 0))).astype(BF16)
        ba_l = gla_ba[l].reshape(1, GLA_QK)
        gnw = gla_norm_w[l].reshape(1, GLA_DV)
        dnw = diff_norm_w[l].reshape(1, DIFF_DV)
        w_out_l = w_out[l].astype(BF16)
        wq_l = cross_wq[l].astype(BF16)
        wk_l = cross_wk[l].astype(BF16)
        wv_l = cross_wv[l].astype(BF16)
        wo_l = cross_wo[l].astype(BF16)
        w_up_l = ffn_w_up[l].astype(BF16)
        w_down_l = ffn_w_down[l].astype(BF16)
        cw_l = ffn_conv_w[l].reshape(CONV_W, 2, D_FF).transpose(1, 0, 2)
        cb_l = ffn_conv_b[l].reshape(2, 1, D_FF)
        r1 = lambda a: a[l].reshape(1, D_MODEL)
        ln = [r1(a) for a in (ln1_g, ln1_b, ln2_g, ln2_b, ln3_g, ln3_b)]

        gq, gk, gg, gv, gr, dq, k_stack, v_stack, dkb, _, dvt = _inproj(
            xp, w_in_l, wa2_l, ba_l, tabs_p, tp // tq, tq, stack=(l, depth, k_stack, v_stack))
        b3 = lambda a: a.reshape(bp, tp, a.shape[-1])
        go, s_p = _gla(b3(gq), b3(gk), b3(gg), b3(gv), b3(gr), gnw, None, bp, gla_tt, GLA_CHUNK, GLA_SUB, None)
        do = _diff_prompt(diff_lambda[l], dnw, b3(dq), b3(dkb),
                          dvt.reshape(bp, tp // tq, DIFF_HEADS, DIFF_DV, tq), tq, lam_init)
        mk_p, mv_p = _memkv(mem2, wk_l, wv_l)
        x2 = _mix_cross(go, do, xp.reshape(bp, tp, D_MODEL), w_out_l, ln[0], ln[1], wq_l,
                        mk_p.reshape(1, bp, MEM_LEN * _MEM_SPLIT, LANES),
                        mv_p.reshape(1, bp, MEM_LEN * _MEM_SPLIT, LANES), 0,
                        wo_l, ln[2], ln[3], tm_p, alpha)
        x3, conv_p = _ffn(x2, zero_conv, w_up_l, cw_l, cb_l, w_down_l, ln[4], ln[5], tm_p, 1, tm_p, alpha)
        xp = x3.reshape(bp * tp, D_MODEL)
        outs[4].append(s_p)
        outs[6].append(conv_p[:, -1].transpose(0, 2, 1, 3, 4).reshape(bp, CONV_W - 1, 2 * D_FF))
        outs[8].append(_mem_rows_unview(mk_p.reshape(bp, MEM_LEN * _MEM_SPLIT, LANES)))
        outs[9].append(_mem_rows_unview(mv_p.reshape(bp, MEM_LEN * _MEM_SPLIT, LANES)))

        gq, gk, gg, gv, gr, dq, dkf, dvf, dkb, dvb, _ = _inproj(xs, w_in_l, wa2_l, ba_l, tabs_s, 1, bs * SAMPLE_PAD)
        s3 = lambda a: a.reshape(bs, SAMPLE_PAD, a.shape[-1])
        go, s_s = _gla(s3(gq), s3(gk), s3(gg), s3(gv), s3(gr), gnw, state_gla[l],
                       math.gcd(bs, 8), SAMPLE_PAD, SAMPLE_PAD, SAMPLE_PAD, ts)
        dq_s = s3(dq)[:, :_QROWS].astype(F32)
        do8 = _diff_sample(page_table, diff_lambda[l], dnw, dq_s, s3(dkb), s3(dvb),
                           cache_k4, cache_v4, l, pages, ts, lam_init)
        do = jnp.pad(do8, ((0, 0), (0, SAMPLE_PAD - _QROWS), (0, 0))).astype(BF16)
        x2 = _mix_cross_split(go, do, xs.reshape(bs, SAMPLE_PAD, D_MODEL), w_out_l, ln[0], ln[1], wq_l,
                              cmem_k, cmem_v, l, wo_l, ln[2], ln[3], alpha)
        x2t = x2.transpose(1, 0, 2).reshape(1, SAMPLE_PAD * bs, D_MODEL)
        x3t, conv_s = _ffn(x2t, _conv_state_in(state_conv[l]), w_up_l, cw_l, cb_l, w_down_l, ln[4], ln[5],
                           SAMPLE_PAD * bs, bs, ts, alpha)
        xs = x3t.reshape(SAMPLE_PAD, bs, D_MODEL).transpose(1, 0, 2).reshape(bs * SAMPLE_PAD, D_MODEL)
        outs[2].append(dkf.reshape(bs, SAMPLE_PAD, DIFF_HEADS, 2 * DIFF_DH)[:, :ts])
        outs[3].append(dvf.reshape(bs, SAMPLE_PAD, DIFF_HEADS, DIFF_DV)[:, :ts])
        outs[5].append(s_s)
        outs[7].append(_conv_state_out(conv_s[:, -1]))

    outs[0] = k_stack.reshape(depth, bp, tp // PAGE_SIZE, PAGE_SIZE, DIFF_HEADS, 2 * DIFF_DH)
    outs[1] = v_stack.reshape(depth, bp, tp // PAGE_SIZE, PAGE_SIZE, DIFF_HEADS, DIFF_DV)
    st = [o if i < 2 else jnp.stack(o) for i, o in enumerate(outs)]
    y_prompt = xp.reshape(bp, tp, D_MODEL)
    y_sample = xs.reshape(bs, SAMPLE_PAD, D_MODEL)[:, :ts]
    return (y_prompt, y_sample, st[0], st[1], st[2], st[3], st[4], st[5], st[6], st[7], st[8], st[9])
```

```python
import functools
import math

import jax
import jax.numpy as jnp
from jax import lax
from jax.experimental import pallas as pl
from jax.experimental.pallas import tpu as pltpu

F32 = jnp.float32
BF16 = jnp.bfloat16

D_MODEL = 1024
PAGE_SIZE = 128
GLA_HEADS = 4
GLA_DK = 64
GLA_DV = 128
GLA_RANK = 16
GLA_TAU = 16.0
GLA_CHUNK = 64
GLA_SUB = 16
DIFF_HEADS = 4
DIFF_DH = 64
DIFF_DV = 128
ROPE_THETA = 10000.0
MEM_LEN = 256
MEM_HEADS = 4
MEM_DH = D_MODEL // MEM_HEADS
D_FF = 2816
CONV_W = 3
LN_EPS = 1e-5
LOG2E = 1.4426950408889634

GLA_QK = GLA_HEADS * GLA_DK
GLA_V = GLA_HEADS * GLA_DV
DIFF_QK = DIFF_HEADS * 2 * DIFF_DH
DIFF_V = DIFF_HEADS * DIFF_DV
LANES = 128
N_IN_PAD = 2 * GLA_QK + 2 * GLA_V + 2 * DIFF_QK + DIFF_V + LANES
SAMPLE_PAD = 16
VMEM_LIMIT = 52 * 1024 * 1024


def _cparams(sem):
    return pltpu.CompilerParams(dimension_semantics=sem, vmem_limit_bytes=VMEM_LIMIT)


def _post_ln(res, sub, g, b, alpha):
    h = alpha * res + sub
    mu = jnp.mean(h, axis=-1, keepdims=True)
    hc = h - mu
    var = jnp.mean(hc * hc, axis=-1, keepdims=True)
    return hc * lax.rsqrt(var + LN_EPS) * g + b


def _dot(a, b):
    return jnp.dot(a, b, preferred_element_type=F32)


def _dot_nt(a, b):
    return lax.dot_general(a, b, (((1,), (1,)), ((), ())), preferred_element_type=F32)


def _dot_tn(a, b):
    return lax.dot_general(a, b, (((0,), (0,)), ((), ())), preferred_element_type=F32)


_O_GQ, _O_GK, _O_GV, _O_GR = 0, GLA_QK, 2 * GLA_QK, 2 * GLA_QK + GLA_V
_O_DQ = 2 * GLA_QK + 2 * GLA_V
_O_DK = _O_DQ + DIFF_QK
_O_DV = _O_DK + DIFF_QK
_O_GA = _O_DV + DIFF_V


def _inproj_kernel(x_ref, w_ref, wa2_ref, ba_ref, c_ref, sa_ref, sb_ref, *rest):
    gq_ref, gk_ref, gg_ref, gv_ref, gr_ref, dq_ref, dkf_ref, dvf_ref, dkb_ref, dvb_ref, dvt_ref = rest[-11:]
    h = _dot(x_ref[...].astype(BF16), w_ref[...])
    gq_ref[...] = h[:, _O_GQ:_O_GQ + GLA_QK] * (GLA_DK ** -0.5)
    gk_ref[...] = h[:, _O_GK:_O_GK + GLA_QK]
    gv_ref[...] = h[:, _O_GV:_O_GV + GLA_V].astype(BF16)
    gr_ref[...] = h[:, _O_GR:_O_GR + GLA_V]
    xg = _dot(h[:, _O_GA:_O_GA + LANES].astype(BF16), wa2_ref[...]) + ba_ref[...]
    gg_ref[...] = (jnp.minimum(xg, 0.0) - jnp.log1p(jnp.exp(-jnp.abs(xg)))) * (1.0 / GLA_TAU)
    c, sa, sb = c_ref[...], sa_ref[...], sb_ref[...]

    def rope(v):
        return v * c + pltpu.roll(v, LANES - DIFF_DH // 2, 1) * sa + pltpu.roll(v, DIFF_DH // 2, 1) * sb

    qscale = (DIFF_DH ** -0.5) * LOG2E
    tm = h.shape[0]
    for j in range(DIFF_HEADS):
        sl = slice(j * LANES, (j + 1) * LANES)
        head_rows = pl.ds(j, tm, stride=DIFF_HEADS)
        dq_ref[:, sl] = (rope(h[:, _O_DQ + j * LANES:_O_DQ + (j + 1) * LANES]) * qscale).astype(BF16)
        rk = rope(h[:, _O_DK + j * LANES:_O_DK + (j + 1) * LANES])
        dkf_ref[head_rows, :] = rk
        dkb_ref[:, sl] = rk.astype(BF16)
        dv = h[:, _O_DV + j * LANES:_O_DV + (j + 1) * LANES]
        dvf_ref[head_rows, :] = dv
        dvb_ref[:, sl] = dv.astype(BF16)
        dvt_ref[j] = dv.T.astype(BF16)


_I_DKF, _I_DVF = 6, 7


def _inproj(x, w, wa2, ba, tabs, tab_blocks, tm, stack=None):
    m = x.shape[0]
    row = lambda n: pl.BlockSpec((tm, n), lambda i: (i, 0))
    full = lambda a: pl.BlockSpec(a.shape, lambda i: (0,) * a.ndim)
    tab = pl.BlockSpec((tm, LANES), lambda i: (i % tab_blocks, 0))
    shapes = [(1, GLA_QK, F32), (1, GLA_QK, F32), (1, GLA_QK, F32), (1, GLA_V, BF16), (1, GLA_V, F32),
              (1, DIFF_QK, BF16), (DIFF_HEADS, LANES, F32), (DIFF_HEADS, LANES, F32),
              (1, DIFF_QK, BF16), (1, DIFF_V, BF16)]
    out_specs = [pl.BlockSpec((tm * k, n), lambda i: (i, 0)) for k, n, _ in shapes]
    out_shape = [jax.ShapeDtypeStruct((m * k, n), dt) for k, n, dt in shapes]
    out_specs.append(pl.BlockSpec((None, DIFF_HEADS, DIFF_DV, tm), lambda i: (i, 0, 0, 0)))
    out_shape.append(jax.ShapeDtypeStruct((m // tm, DIFF_HEADS, DIFF_DV, tm), BF16))
    in_specs = [row(D_MODEL), full(w), full(wa2), full(ba), tab, tab, tab]
    args = [x, w, wa2, ba, *tabs]
    aliases = {}
    if stack is not None:
        layer, depth, kbuf, vbuf = stack
        for idx, buf in ((_I_DKF, kbuf), (_I_DVF, vbuf)):
            k, n, dt = shapes[idx]
            out_specs[idx] = pl.BlockSpec((None, tm * k, n), lambda i: (layer, i, 0))
            out_shape[idx] = jax.ShapeDtypeStruct((depth, m * k, n), dt)
            if buf is not None:
                aliases[len(args)] = idx
                in_specs.append(pl.BlockSpec(memory_space=pl.ANY))
                args.append(buf)
    return pl.pallas_call(
        _inproj_kernel,
        grid=(m // tm,),
        in_specs=in_specs,
        out_specs=out_specs,
        out_shape=out_shape,
        input_output_aliases=aliases,
        compiler_params=_cparams(("parallel",)),
        name="inproj",
    )(*args)


def _bdot(a, b, contract):
    return lax.dot_general(a, b, (((contract[0],), (contract[1],)), ((0,), (0,))), preferred_element_type=F32)


def _gla_chunk(qc, kc, gc, vc, st, tril, head_a, st_mask, chunk, sub):
    n = qc.shape[0]
    nsub = chunk // sub
    g1 = gc.astype(BF16)
    r1 = gc - g1.astype(F32)
    g2 = r1.astype(BF16)
    g3 = (r1 - g2.astype(F32)).astype(BF16)
    b3 = _dot(tril, jnp.concatenate([g[i] for i in range(n) for g in (g1, g2, g3)], axis=1))
    b = jnp.stack([b3[:, (3 * i) * LANES:(3 * i + 1) * LANES] + b3[:, (3 * i + 1) * LANES:(3 * i + 2) * LANES]
                   + b3[:, (3 * i + 2) * LANES:(3 * i + 3) * LANES] for i in range(n)])
    b_last = b[:, chunk - 1:chunk, :]

    def by_head(x):
        return jnp.concatenate([jnp.where(head_a, x, 0.0), jnp.where(head_a, 0.0, x)], axis=1).astype(BF16)

    oi = _bdot(by_head(qc * jnp.exp(b)), st.astype(BF16), (2, 2))
    o_inter = jnp.concatenate([oi[:, 0:chunk, 0:GLA_DV], oi[:, chunk:, GLA_DV:]], axis=2)
    key_row = lax.broadcasted_iota(jnp.int32, (1, chunk, 1), 1)
    lqs, kss = [], []
    for s in range(nsub):
        r0, r1_ = s * sub, (s + 1) * sub
        anchor = b[:, r0:r0 + 1, :]
        lqs.append(by_head(qc[:, r0:r1_] * jnp.exp(b[:, r0:r1_] - anchor)))
        kss.append((kc * jnp.exp(jnp.where(key_row < r1_, anchor - b, 0.0))).astype(BF16))
    att = _bdot(jnp.concatenate(lqs, axis=0), jnp.concatenate(kss, axis=0), (2, 2))
    sub_idx = lax.broadcasted_iota(jnp.int32, att.shape, 0) // n
    q_tok = sub_idx * sub + lax.broadcasted_iota(jnp.int32, att.shape, 1) % sub
    att = jnp.where(lax.broadcasted_iota(jnp.int32, att.shape, 2) <= q_tok, att, 0.0).astype(BF16)
    att = jnp.concatenate([att[s * n:(s + 1) * n] for s in range(nsub)], axis=1)
    ov = _bdot(att, vc, (2, 1))
    o_intra = jnp.concatenate(
        [jnp.concatenate([ov[:, 2 * s * sub:(2 * s + 1) * sub, 0:GLA_DV],
                          ov[:, (2 * s + 1) * sub:(2 * s + 2) * sub, GLA_DV:]], axis=2) for s in range(nsub)], axis=1)
    upd = _bdot(vc, (kc * jnp.exp(b_last - b)).astype(BF16), (1, 1))
    return o_inter + o_intra, st * jnp.exp(b_last) + jnp.where(st_mask, upd, 0.0)


def _gla_kernel(*refs, bb, tt, chunk, sub, t_valid, has_s0):
    if has_s0:
        q_ref, k_ref, g_ref, v_ref, gr_ref, nw_ref, s0_ref, o_ref, s_out_ref, st_scr = refs
    else:
        q_ref, k_ref, g_ref, v_ref, gr_ref, nw_ref, o_ref, s_out_ref, st_scr = refs
    i = pl.program_id(1)
    pairs = GLA_HEADS // 2
    head_a = lax.broadcasted_iota(jnp.int32, (1, LANES), 1) < GLA_DK
    st_mask = (lax.broadcasted_iota(jnp.int32, (2 * GLA_DV, LANES), 0) // GLA_DV
               == lax.broadcasted_iota(jnp.int32, (2 * GLA_DV, LANES), 1) // GLA_DK)
    tril = (lax.broadcasted_iota(jnp.int32, (chunk, chunk), 0)
            >= lax.broadcasted_iota(jnp.int32, (chunk, chunk), 1)).astype(BF16)
    row_in_chunk = lax.broadcasted_iota(jnp.int32, (chunk, 1), 0)

    @pl.when(i == 0)
    def _():
        for bi in range(bb):
            for p in range(pairs):
                if has_s0:
                    s0t = s0_ref[bi, 2 * p:2 * p + 2].reshape(2 * GLA_DK, GLA_DV).T
                    st_scr[bi, p] = jnp.where(st_mask, jnp.concatenate([s0t, s0t], axis=0), 0.0)
                else:
                    st_scr[bi, p] = jnp.zeros((2 * GLA_DV, LANES), F32)

    nw = nw_ref[...]
    chains = [(bi, p) for bi in range(bb) for p in range(pairs)]
    ks = lambda p: slice(p * LANES, (p + 1) * LANES)
    vs = lambda p: slice(p * 2 * GLA_DV, (p + 1) * 2 * GLA_DV)
    st = st_scr[...].reshape(len(chains), 2 * GLA_DV, LANES)
    for c in range(tt // chunk):
        rs = slice(c * chunk, (c + 1) * chunk)
        qc = jnp.stack([q_ref[bi, rs, ks(p)] for bi, p in chains])
        kc = jnp.stack([k_ref[bi, rs, ks(p)] for bi, p in chains])
        gc = jnp.stack([g_ref[bi, rs, ks(p)] for bi, p in chains])
        vc = jnp.stack([v_ref[bi, rs, vs(p)] for bi, p in chains])
        if t_valid is not None:
            ok = (row_in_chunk + c * chunk) < t_valid
            kc = jnp.where(ok, kc, 0.0)
            gc = jnp.where(ok, gc, 0.0)
            vc = jnp.where(ok, vc, jnp.zeros_like(vc))
        o, st = _gla_chunk(qc, kc, gc, vc, st, tril, head_a, st_mask, chunk, sub)
        gr = jnp.stack([gr_ref[bi, rs, vs(p)] for bi, p in chains])
        normed = []
        for hh in range(2):
            oh = o[:, :, hh * GLA_DV:(hh + 1) * GLA_DV]
            normed.append(oh * lax.rsqrt(jnp.mean(oh * oh, axis=-1, keepdims=True) + LN_EPS) * nw)
        out = (jnp.concatenate(normed, axis=2) * (gr * (1.0 / (1.0 + jnp.exp(-gr))))).astype(o_ref.dtype)
        for idx, (bi, p) in enumerate(chains):
            o_ref[bi, rs, vs(p)] = out[idx]
    st_scr[...] = st.reshape(st_scr.shape)

    @pl.when(i == pl.num_programs(1) - 1)
    def _():
        for bi in range(bb):
            for p in range(pairs):
                st = st_scr[bi, p]
                sa = st[0:GLA_DV].T
                sb = st[GLA_DV:].T
                s_out_ref[bi, 2 * p:2 * p + 2] = jnp.concatenate(
                    [sa[0:GLA_DK], sb[GLA_DK:]], axis=0).reshape(2, GLA_DK, GLA_DV)


def _gla(q, k, g, v, gr, nw, s0, bb, tt, chunk, sub, t_valid):
    bsz, t, _ = q.shape
    has_s0 = s0 is not None
    qk_spec = pl.BlockSpec((bb, tt, GLA_QK), lambda b, i: (b, i, 0))
    v_spec = pl.BlockSpec((bb, tt, GLA_V), lambda b, i: (b, i, 0))
    s_spec = pl.BlockSpec((bb, GLA_HEADS, GLA_DK, GLA_DV), lambda b, i: (b, 0, 0, 0))
    in_specs = [qk_spec, qk_spec, qk_spec, v_spec, v_spec, pl.BlockSpec(nw.shape, lambda b, i: (0, 0))]
    args = [q, k, g, v, gr, nw]
    if has_s0:
        in_specs.append(s_spec)
        args.append(s0)
    return pl.pallas_call(
        functools.partial(_gla_kernel, bb=bb, tt=tt, chunk=chunk, sub=sub, t_valid=t_valid, has_s0=has_s0),
        grid=(bsz // bb, t // tt),
        in_specs=in_specs,
        out_specs=[v_spec, s_spec],
        out_shape=[jax.ShapeDtypeStruct((bsz, t, GLA_V), BF16),
                   jax.ShapeDtypeStruct((bsz, GLA_HEADS, GLA_DK, GLA_DV), F32)],
        scratch_shapes=[pltpu.VMEM((bb, GLA_HEADS // 2, 2 * GLA_DV, LANES), F32)],
        compiler_params=_cparams(("parallel", "arbitrary")),
        name="gla",
    )(*args)


def _diff_lambda(lam_ref, lam_init):
    lf = lam_ref[...]
    a = jnp.sum(lf[0:1] * lf[1:2], axis=-1, keepdims=True)
    b = jnp.sum(lf[2:3] * lf[3:4], axis=-1, keepdims=True)
    return jnp.exp(a) - jnp.exp(b) + lam_init


def _diff_finish(o0, o1, lam, nw, lam_init):
    od = o0 - lam * o1
    ms = jnp.mean(od * od, axis=-1, keepdims=True)
    return od * lax.rsqrt(ms + LN_EPS) * nw * (1.0 - lam_init)


_ONES_ROWS = 16


def _diff_prompt_kernel(lam_ref, nw_ref, q_ref, k_ref, vt_ref, o_ref,
                        qz_scr, s_scr, p_scr, m_scr, a_scr, acc_scr, *, tq, lam_init):
    qi = pl.program_id(2)
    q = q_ref[...]
    lane = lax.broadcasted_iota(jnp.int32, (1, LANES), 1)
    zero = jnp.zeros_like(q)
    qz_scr[0:tq, :] = jnp.where(lane < DIFF_DH, q, zero)
    qz_scr[tq:, :] = jnp.where(lane < DIFF_DH, zero, q)
    m_scr[...] = jnp.full_like(m_scr, -jnp.inf)
    acc_scr[...] = jnp.zeros_like(acc_scr)
    ones = jnp.ones((_ONES_ROWS, tq), BF16)
    key_row = lax.broadcasted_iota(jnp.int32, (tq, 1), 0)

    def scores(t, slot):
        ks = pl.multiple_of(t * tq, tq)
        s_scr[slot] = _dot_nt(k_ref[pl.ds(ks, tq), :], qz_scr[...])

    def softmax(slot, masked):
        for c in range(2 * tq // LANES):
            cols = slice(c * LANES, (c + 1) * LANES)
            s = s_scr[slot, :, cols]
            if masked:
                s = jnp.where(key_row <= lane + (c * LANES) % tq, s, -jnp.inf)
            m_prev = m_scr[:, cols]
            m_new = jnp.maximum(m_prev, jnp.max(s, axis=0, keepdims=True))
            a_scr[slot, :, cols] = jnp.exp2(m_prev - m_new)
            m_scr[:, cols] = m_new
            p_scr[slot, :, cols] = jnp.exp2(s - m_new[0:1]).astype(BF16)

    def values(t, slot):
        v1t = jnp.concatenate([vt_ref[t], ones], axis=0)
        acc_scr[...] = acc_scr[...] * a_scr[slot, 0:1, :] + _dot(v1t, p_scr[slot])

    def step(t, slot):
        scores(t + 1, 1 - slot)
        softmax(slot, False)
        values(jnp.maximum(t - 1, 0), 1 - slot)

    def idle(slot):
        p_scr[slot] = jnp.zeros(p_scr.shape[1:], BF16)
        a_scr[slot] = jnp.ones(a_scr.shape[1:], F32)

    off = qi % 2

    @pl.when(off == 0)
    def _():
        idle(1)
        scores(0, 0)

    @pl.when(off == 1)
    def _():
        idle(0)
        scores(0, 1)
        step(0, 1)

    def body(u, carry):
        step(2 * u + off, 0)
        step(2 * u + 1 + off, 1)
        return carry

    lax.fori_loop(0, qi // 2, body, 0)
    softmax(0, True)
    values(jnp.maximum(qi - 1, 0), 1)
    values(qi, 0)

    acc = acc_scr[...]
    o = acc[0:DIFF_DV, :] / acc[DIFF_DV:DIFF_DV + 1, :]
    lam = _diff_lambda(lam_ref, lam_init)
    o_ref[...] = _diff_finish(o[:, 0:tq].T, o[:, tq:].T, lam, nw_ref[...], lam_init).astype(o_ref.dtype)


def _diff_prompt(lam_p, nw, q, k, vt, tq, lam_init):
    bsz, t, _ = q.shape
    qspec = pl.BlockSpec((None, tq, LANES), lambda b, h, i: (b, i, h))
    kspec = pl.BlockSpec((None, t, LANES), lambda b, h, i: (b, 0, h))
    vspec = pl.BlockSpec((None, t // tq, None, DIFF_DV, tq), lambda b, h, i: (b, 0, h, 0, 0))
    small = lambda a: pl.BlockSpec(a.shape, lambda b, h, i: (0,) * a.ndim)
    return pl.pallas_call(
        functools.partial(_diff_prompt_kernel, tq=tq, lam_init=lam_init),
        grid=(bsz, DIFF_HEADS, t // tq),
        in_specs=[small(lam_p), small(nw), qspec, kspec, vspec],
        out_specs=qspec,
        out_shape=jax.ShapeDtypeStruct((bsz, t, DIFF_V), BF16),
        scratch_shapes=[pltpu.VMEM((2 * tq, LANES), BF16),
                        pltpu.VMEM((2, tq, 2 * tq), F32),
                        pltpu.VMEM((2, tq, 2 * tq), BF16),
                        pltpu.VMEM((8, 2 * tq), F32),
                        pltpu.VMEM((2, 8, 2 * tq), F32),
                        pltpu.VMEM((DIFF_DV + _ONES_ROWS, 2 * tq), F32)],
        compiler_params=_cparams(("parallel", "parallel", "arbitrary")),
        name="diff_prompt",
    )(lam_p, nw, q, k, vt)


_QROWS = 8


def _diff_sample_kernel(pt_ref, lam_ref, nw_ref, q_ref, kn_ref, vn_ref, *rest, pages, t_new, lam_init):
    k_refs, v_refs = rest[:pages], rest[pages:2 * pages]
    o_ref, qz_scr, m_scr, l_scr, acc_scr = rest[2 * pages:]
    j = pl.program_id(1)
    grp = 2 * _QROWS

    def update(keys, vals, mask):
        s = jnp.concatenate([_dot_nt(qz_scr[h * grp:(h + 1) * grp, :], keys[h]) for h in range(DIFF_HEADS)], axis=0)
        if mask is not None:
            s = jnp.where(mask, s, -jnp.inf)
        m_prev = m_scr[:, 0:1]
        m_new = jnp.maximum(m_prev, jnp.max(s, axis=1, keepdims=True))
        alpha = jnp.exp2(m_prev - m_new)
        p = jnp.exp2(s - m_new)
        l_new = l_scr[:, 0:1] * alpha + jnp.sum(p, axis=1, keepdims=True)
        pb = p.astype(BF16)
        pv = jnp.concatenate([_dot(pb[h * grp:(h + 1) * grp], vals[h]) for h in range(DIFF_HEADS)], axis=0)
        l_scr[...] = jnp.broadcast_to(l_new, l_scr.shape)
        acc_scr[...] = acc_scr[...] * alpha + pv
        m_scr[...] = jnp.broadcast_to(m_new, m_scr.shape)

    @pl.when(j == 0)
    def _():
        q = q_ref[...]
        lane = lax.broadcasted_iota(jnp.int32, (1, LANES), 1)
        m_scr[...] = jnp.full_like(m_scr, -jnp.inf)
        l_scr[...] = jnp.zeros_like(l_scr)
        acc_scr[...] = jnp.zeros_like(acc_scr)
        for h in range(DIFF_HEADS):
            qh = q[:, h * LANES:(h + 1) * LANES]
            qz_scr[h * grp:(h + 1) * grp, :] = jnp.concatenate(
                [jnp.where(lane < DIFF_DH, qh, 0.0), jnp.where(lane < DIFF_DH, 0.0, qh)], axis=0).astype(BF16)
        rr = lax.broadcasted_iota(jnp.int32, (DIFF_HEADS * grp, SAMPLE_PAD), 0) % _QROWS
        cc = lax.broadcasted_iota(jnp.int32, (DIFF_HEADS * grp, SAMPLE_PAD), 1)
        update([kn_ref[:, h * LANES:(h + 1) * LANES] for h in range(DIFF_HEADS)],
               [vn_ref[:, h * LANES:(h + 1) * LANES] for h in range(DIFF_HEADS)], (cc <= rr) & (cc < t_new))

    def head_of(refs, h):
        return jnp.concatenate([r[pl.ds(h, PAGE_SIZE, stride=DIFF_HEADS), :].astype(BF16) for r in refs], axis=0)

    update([head_of(k_refs, h) for h in range(DIFF_HEADS)], [head_of(v_refs, h) for h in range(DIFF_HEADS)], None)

    @pl.when(j == pl.num_programs(1) - 1)
    def _():
        acc = acc_scr[...]
        inv = 1.0 / l_scr[:, 0:1]
        lam = _diff_lambda(lam_ref, lam_init)
        nw = nw_ref[...]
        outs = []
        for h in range(DIFF_HEADS):
            r0 = h * grp
            r1 = r0 + _QROWS
            o0 = acc[r0:r0 + _QROWS] * inv[r0:r0 + _QROWS]
            o1 = acc[r1:r1 + _QROWS] * inv[r1:r1 + _QROWS]
            outs.append(_diff_finish(o0, o1, lam, nw, lam_init))
        o_ref[...] = jnp.concatenate(outs, axis=1)


def _diff_sample(page_table, lam_p, nw, q, k_new, v_new, cache_k, cache_v, layer, pages, t_new, lam_init):
    bsz, n_pages = page_table.shape
    small = lambda a: pl.BlockSpec(a.shape, lambda b, j, pt: (0,) * a.ndim)
    qspec = pl.BlockSpec((None, _QROWS, DIFF_QK), lambda b, j, pt: (b, 0, 0))
    nspec = pl.BlockSpec((None, SAMPLE_PAD, DIFF_QK), lambda b, j, pt: (b, 0, 0))

    def page_spec(i):
        return pl.BlockSpec((None, None, PAGE_SIZE * DIFF_HEADS, LANES),
                            lambda b, j, pt: (layer, pt[b, j * pages + i], 0, 0))

    nrow = 2 * DIFF_HEADS * _QROWS
    grid_spec = pltpu.PrefetchScalarGridSpec(
        num_scalar_prefetch=1,
        grid=(bsz, n_pages // pages),
        in_specs=[small(lam_p), small(nw), qspec, nspec, nspec]
                 + [page_spec(i) for i in range(pages)] + [page_spec(i) for i in range(pages)],
        out_specs=qspec,
        scratch_shapes=[pltpu.VMEM((nrow, LANES), BF16), pltpu.VMEM((nrow, LANES), F32),
                        pltpu.VMEM((nrow, LANES), F32), pltpu.VMEM((nrow, DIFF_DV), F32)],
    )
    return pl.pallas_call(
        functools.partial(_diff_sample_kernel, pages=pages, t_new=t_new, lam_init=lam_init),
        grid_spec=grid_spec,
        out_shape=jax.ShapeDtypeStruct((bsz, _QROWS, DIFF_V), F32),
        compiler_params=_cparams(("parallel", "arbitrary")),
        name="diff_sample",
    )(page_table, lam_p, nw, q, k_new, v_new, *([cache_k] * pages), *([cache_v] * pages))


_MEM_SPLIT = D_MODEL // LANES
_MEM_PIECES = MEM_DH // LANES


def _mem_row(head, piece):
    return piece * MEM_HEADS + head


def _mem_rows_view(a):
    lead = a.shape[:-3]
    n = len(lead)
    a = a.reshape(*lead, MEM_LEN, MEM_HEADS, _MEM_PIECES, LANES)
    a = a.transpose(*range(n), n, n + 2, n + 1, n + 3)
    return a.reshape(*lead, MEM_LEN * _MEM_SPLIT, LANES)


def _mem_rows_unview(a):
    lead = a.shape[:-2]
    n = len(lead)
    a = a.reshape(*lead, MEM_LEN, _MEM_PIECES, MEM_HEADS, LANES)
    a = a.transpose(*range(n), n, n + 2, n + 1, n + 3)
    return a.reshape(*lead, MEM_LEN, MEM_HEADS, MEM_DH)


def _memkv_kernel(x_ref, wk_ref, wv_ref, k_ref, v_ref):
    x = x_ref[...].astype(BF16)
    m = x.shape[0]
    for w_ref, o_ref in ((wk_ref, k_ref), (wv_ref, v_ref)):
        y = _dot(x, w_ref[...])
        for j in range(_MEM_SPLIT):
            o_ref[pl.ds(_mem_row(j // _MEM_PIECES, j % _MEM_PIECES), m, stride=_MEM_SPLIT), :] = (
                y[:, j * LANES:(j + 1) * LANES])


def _memkv(mem, wk, wv):
    m = mem.shape[0]
    full = lambda a: pl.BlockSpec(a.shape, lambda i: (0, 0))
    out = pl.BlockSpec((m * _MEM_SPLIT, LANES), lambda i: (0, 0))
    return pl.pallas_call(
        _memkv_kernel,
        grid=(1,),
        in_specs=[full(mem), full(wk), full(wv)],
        out_specs=[out, out],
        out_shape=[jax.ShapeDtypeStruct((m * _MEM_SPLIT, LANES), F32)] * 2,
        compiler_params=_cparams(("arbitrary",)),
        name="memkv",
    )(mem, wk, wv)


def _mix_proj(g_ref, d_ref, x_ref, wo1_ref, ln1g_ref, ln1b_ref, wq_ref, alpha):
    mix = _dot(g_ref[...], wo1_ref[0:GLA_V, :]) + _dot(d_ref[...], wo1_ref[GLA_V:, :])
    x1 = _post_ln(x_ref[...], mix, ln1g_ref[...], ln1b_ref[...], alpha)
    return x1, (_dot(x1.astype(BF16), wq_ref[...]) * ((MEM_DH ** -0.5) * LOG2E)).astype(BF16)


def _cross_heads(q, mk_ref, mv_ref):
    heads = []
    for h in range(MEM_HEADS):
        def head_of(ref):
            return jnp.concatenate([ref[pl.ds(_mem_row(h, j), MEM_LEN, stride=_MEM_SPLIT), :]
                                    for j in range(_MEM_PIECES)], axis=1).astype(BF16)

        s = _dot_nt(q[:, h * MEM_DH:(h + 1) * MEM_DH], head_of(mk_ref))
        p = jnp.exp2(s - jnp.max(s, axis=-1, keepdims=True))
        l = jnp.sum(p, axis=-1, keepdims=True)
        heads.append((_dot(p.astype(BF16), head_of(mv_ref)) / l).astype(BF16))
    return jnp.concatenate(heads, axis=1)


def _mix_cross_kernel(g_ref, d_ref, x_ref, wo1_ref, ln1g_ref, ln1b_ref, wq_ref, mk_ref, mv_ref, wo2_ref,
                      ln2g_ref, ln2b_ref, o_ref, *, alpha):
    x1, q = _mix_proj(g_ref, d_ref, x_ref, wo1_ref, ln1g_ref, ln1b_ref, wq_ref, alpha)
    y = _dot(_cross_heads(q, mk_ref, mv_ref), wo2_ref[...])
    o_ref[...] = _post_ln(x1, y, ln2g_ref[...], ln2b_ref[...], alpha)


def _mix_proj_kernel(g_ref, d_ref, x_ref, wo1_ref, ln1g_ref, ln1b_ref, wq_ref, x1_ref, q_ref, *, alpha):
    x1_ref[...], q_ref[...] = _mix_proj(g_ref, d_ref, x_ref, wo1_ref, ln1g_ref, ln1b_ref, wq_ref, alpha)


def _cross_kernel(q_ref, mk_ref, mv_ref, o_ref):
    o_ref[...] = _cross_heads(q_ref[...], mk_ref, mv_ref)


def _out_proj_kernel(o_ref, x1_ref, wo2_ref, ln2g_ref, ln2b_ref, y_ref, *, alpha):
    y_ref[...] = _post_ln(x1_ref[...], _dot(o_ref[...], wo2_ref[...]), ln2g_ref[...], ln2b_ref[...], alpha)


def _mix_cross_split(g, d, x, w_out, ln1g, ln1b, wq, mk, mv, mem_layer, wo, ln2g, ln2b, alpha):
    bsz, r, _ = x.shape
    rows = bsz * r
    flat = lambda a: a.reshape(rows, a.shape[-1])
    full = lambda a: pl.BlockSpec(a.shape, lambda *_: (0,) * a.ndim)
    x1, q = pl.pallas_call(
        functools.partial(_mix_proj_kernel, alpha=alpha),
        grid=(1,),
        in_specs=[full(flat(g)), full(flat(d)), full(flat(x)), full(w_out), full(ln1g), full(ln1b), full(wq)],
        out_specs=[full(flat(x)), full(flat(x))],
        out_shape=[jax.ShapeDtypeStruct((rows, D_MODEL), F32), jax.ShapeDtypeStruct((rows, D_MODEL), BF16)],
        compiler_params=_cparams(("arbitrary",)),
        name="mix_proj",
    )(flat(g), flat(d), flat(x), w_out, ln1g, ln1b, wq)
    seq = pl.BlockSpec((None, r, D_MODEL), lambda b: (b, 0, 0))
    mem = pl.BlockSpec((None, None, MEM_LEN * _MEM_SPLIT, LANES), lambda b: (mem_layer, b, 0, 0))
    o = pl.pallas_call(
        _cross_kernel,
        grid=(bsz,),
        in_specs=[seq, mem, mem],
        out_specs=seq,
        out_shape=jax.ShapeDtypeStruct((bsz, r, D_MODEL), BF16),
        compiler_params=_cparams(("parallel",)),
        name="cross",
    )(q.reshape(bsz, r, D_MODEL), mk, mv)
    y = pl.pallas_call(
        functools.partial(_out_proj_kernel, alpha=alpha),
        grid=(1,),
        in_specs=[full(flat(x)), full(flat(x)), full(wo), full(ln2g), full(ln2b)],
        out_specs=full(flat(x)),
        out_shape=jax.ShapeDtypeStruct((rows, D_MODEL), F32),
        compiler_params=_cparams(("arbitrary",)),
        name="out_proj",
    )(flat(o), x1, wo, ln2g, ln2b)
    return y.reshape(bsz, r, D_MODEL)


def _mix_cross(g, d, x, w_out, ln1g, ln1b, wq, mk, mv, mem_layer, wo, ln2g, ln2b, tm, alpha):
    bsz, r, _ = x.shape
    row = lambda n: pl.BlockSpec((None, tm, n), lambda b, i: (b, i, 0))
    full = lambda a: pl.BlockSpec(a.shape, lambda b, i: (0,) * a.ndim)
    mem = pl.BlockSpec((None, None, MEM_LEN * _MEM_SPLIT, LANES), lambda b, i: (mem_layer, b, 0, 0))
    return pl.pallas_call(
        functools.partial(_mix_cross_kernel, alpha=alpha),
        grid=(bsz, r // tm),
        in_specs=[row(GLA_V), row(DIFF_V), row(D_MODEL), full(w_out), full(ln1g), full(ln1b), full(wq),
                  mem, mem, full(wo), full(ln2g), full(ln2b)],
        out_specs=row(D_MODEL),
        out_shape=jax.ShapeDtypeStruct(x.shape, F32),
        compiler_params=_cparams(("parallel", "parallel")),
        name="mix_cross",
    )(g, d, x, w_out, ln1g, ln1b, wq, mk, mv, wo, ln2g, ln2b)


_CARRY = 8
_FF_CHUNK = 256


def _ffn_kernel(x_ref, st_ref, wua_ref, wub_ref, cw_ref, cb_ref, wd_ref, g_ref, b_ref,
                o_ref, ns_ref, carry_scr, h_scr, *, tm, stride, n_valid, alpha):
    i = pl.program_id(1)
    x = x_ref[...]
    xb = x.astype(BF16)
    rows8 = lax.broadcasted_iota(jnp.int32, (_CARRY, 1), 0)

    if stride == 1:
        @pl.when(i == 0)
        def _():
            for half in range(2):
                carry_scr[half, _CARRY - 2:_CARRY - 1, :] = st_ref[half, 0]
                carry_scr[half, _CARRY - 1:_CARRY, :] = st_ref[half, 1]

    for c in range(D_FF // _FF_CHUNK):
        cs = slice(c * _FF_CHUNK, (c + 1) * _FF_CHUNK)
        halves = []
        for half, w_ref in enumerate((wua_ref, wub_ref)):
            u = _dot(xb, w_ref[:, cs])
            if stride == 1:
                p1 = carry_scr[half, _CARRY - 1:_CARRY, cs]
                p2 = carry_scr[half, _CARRY - 2:_CARRY - 1, cs]
                r1, r2 = pltpu.roll(u, 1, 0), pltpu.roll(u, 2, 0)
                u1 = jnp.concatenate([jnp.where(rows8 == 0, p1, r1[0:_CARRY]), r1[_CARRY:]], axis=0)
                u2 = jnp.concatenate(
                    [jnp.where(rows8 == 0, p2, jnp.where(rows8 == 1, p1, r2[0:_CARRY])), r2[_CARRY:]], axis=0)
                carry_scr[half, :, cs] = u[tm - _CARRY:tm, :]
            else:
                padded = jnp.concatenate([st_ref[half, 0, :, cs], st_ref[half, 1, :, cs], u], axis=0)
                u1 = padded[stride:stride + tm]
                u2 = padded[0:tm]
            cw = cw_ref[half, :, cs]
            halves.append(cb_ref[half, :, cs] + cw[0:1] * u2 + cw[1:2] * u1 + cw[2:3] * u)
            for j in range(CONV_W - 1):
                r0 = (n_valid - (CONV_W - 1) + j) * stride
                ns_ref[half, j, :, cs] = u[r0:r0 + stride, :]
        a, bb = halves
        h_scr[:, cs] = (a * (1.0 / (1.0 + jnp.exp(-a))) * bb).astype(BF16)
    o_ref[...] = _post_ln(x, _dot(h_scr[...], wd_ref[...]), g_ref[...], b_ref[...], alpha)


def _ffn(x, state, w_up, conv_w, conv_b, w_down, g, b, tm, stride, n_valid, alpha):
    bsz, r, _ = x.shape
    assert stride == 1 or r == tm
    row = pl.BlockSpec((None, tm, D_MODEL), lambda bi, i: (bi, i, 0))
    st = pl.BlockSpec((None, 2, CONV_W - 1, stride, D_FF), lambda bi, i: (bi, 0, 0, 0, 0))
    once = pl.Buffered(1)
    full = lambda a: pl.BlockSpec(a.shape, lambda bi, i: (0,) * a.ndim, pipeline_mode=once)
    return pl.pallas_call(
        functools.partial(_ffn_kernel, tm=tm, stride=stride, n_valid=n_valid, alpha=alpha),
        grid=(bsz, r // tm),
        in_specs=[row, st,
                  pl.BlockSpec((D_MODEL, D_FF), lambda bi, i: (0, 0), pipeline_mode=once),
                  pl.BlockSpec((D_MODEL, D_FF), lambda bi, i: (0, 1), pipeline_mode=once),
                  full(conv_w), full(conv_b), full(w_down), full(g), full(b)],
        out_specs=[row, pl.BlockSpec((None, None, 2, CONV_W - 1, stride, D_FF), lambda bi, i: (bi, i, 0, 0, 0, 0))],
        out_shape=[jax.ShapeDtypeStruct(x.shape, F32),
                   jax.ShapeDtypeStruct((bsz, r // tm, 2, CONV_W - 1, stride, D_FF), F32)],
        scratch_shapes=[pltpu.VMEM((2, _CARRY, D_FF), F32), pltpu.VMEM((tm, D_FF), BF16)],
        compiler_params=_cparams(("parallel", "arbitrary")),
        name="ffn",
    )(x, state, w_up, w_up, conv_w, conv_b, w_down, g, b)


def _rope_tables(pos):
    half = DIFF_DH // 2
    inv = ROPE_THETA ** (-jnp.arange(half, dtype=F32) / half)
    ang = pos.astype(F32)[:, None] * inv[None, :]
    cos = jnp.tile(jnp.cos(ang), (1, LANES // half))
    sin = jnp.tile(jnp.sin(ang), (1, LANES // half))
    first = (jnp.arange(LANES) % DIFF_DH) < half
    return cos, jnp.where(first, -sin, 0.0), jnp.where(first, 0.0, sin)


def _conv_state_in(s):
    return s.reshape(s.shape[0], CONV_W - 1, 2, D_FF).transpose(2, 1, 0, 3)[None]


def _conv_state_out(s):
    return s[0].transpose(2, 1, 0, 3).reshape(s.shape[3], CONV_W - 1, 2 * D_FF)


def kernel(x_prompt, x_sample, mem_prompt, cache_k, cache_v, page_table, cache_mem_k, cache_mem_v,
           state_gla, state_conv, w_in, gla_wa2, gla_ba, gla_norm_w, diff_lambda, diff_norm_w, w_out,
           ln1_g, ln1_b, cross_wq, cross_wk, cross_wv, cross_wo, ln2_g, ln2_b,
           ffn_w_up, ffn_conv_w, ffn_conv_b, ffn_w_down, ln3_g, ln3_b):
    depth = w_in.shape[0]
    bp, tp, _ = x_prompt.shape
    bs, ts, _ = x_sample.shape
    n_pages = page_table.shape[1]
    past_len = n_pages * PAGE_SIZE
    alpha = (2 * depth) ** 0.25
    assert ts <= _QROWS and tp % 512 == 0

    tm_p = 512
    tq = 512
    gla_tt = 512
    pages = math.gcd(n_pages, 16)

    tabs_p = _rope_tables(jnp.arange(tp))
    pos_s = past_len + jnp.minimum(jnp.arange(SAMPLE_PAD), ts - 1)
    tabs_s = _rope_tables(jnp.tile(pos_s, bs))

    xp = x_prompt.reshape(bp * tp, D_MODEL)
    xs = jnp.pad(x_sample, ((0, 0), (0, SAMPLE_PAD - ts), (0, 0))).reshape(bs * SAMPLE_PAD, D_MODEL)
    mem2 = mem_prompt.reshape(bp * MEM_LEN, D_MODEL)
    zero_conv = jnp.zeros((bp, 2, CONV_W - 1, 1, D_FF), F32)
    n_phys = cache_k.shape[1]
    cache_k4 = cache_k.reshape(depth, n_phys, PAGE_SIZE * DIFF_HEADS, 2 * DIFF_DH)
    cache_v4 = cache_v.reshape(depth, n_phys, PAGE_SIZE * DIFF_HEADS, DIFF_DV)
    cmem_k = _mem_rows_view(cache_mem_k)
    cmem_v = _mem_rows_view(cache_mem_v)

    outs = [[] for _ in range(10)]
    k_stack = v_stack = None
    for l in range(depth):
        lam_init = 0.8 - 0.6 * math.exp(-0.3 * l)
        wl = w_in[l]
        ga0 = 2 * GLA_QK + 2 * GLA_V
        w_in_l = jnp.concatenate(
            [wl[:, :ga0], wl[:, ga0 + GLA_RANK:], wl[:, ga0:ga0 + GLA_RANK],
             jnp.zeros((D_MODEL, LANES - GLA_RANK), F32)], axis=1).astype(BF16)
        wa2_l = jnp.pad(gla_wa2[l], ((0, LANES - GLA_RANK), (0, 0))).astype(BF16)
        ba_l = gla_ba[l].reshape(1, GLA_QK)
        gnw = gla_norm_w[l].reshape(1, GLA_DV)
        dnw = diff_norm_w[l].reshape(1, DIFF_DV)
        w_out_l = w_out[l].astype(BF16)
        wq_l = cross_wq[l].astype(BF16)
        wk_l = cross_wk[l].astype(BF16)
        wv_l = cross_wv[l].astype(BF16)
        wo_l = cross_wo[l].astype(BF16)
        w_up_l = ffn_w_up[l].astype(BF16)
        w_down_l = ffn_w_down[l].astype(BF16)
        cw_l = ffn_conv_w[l].reshape(CONV_W, 2, D_FF).transpose(1, 0, 2)
        cb_l = ffn_conv_b[l].reshape(2, 1, D_FF)
        r1 = lambda a: a[l].reshape(1, D_MODEL)
        ln = [r1(a) for a in (ln1_g, ln1_b, ln2_g, ln2_b, ln3_g, ln3_b)]

        gq, gk, gg, gv, gr, dq, k_stack, v_stack, dkb, _, dvt = _inproj(
            xp, w_in_l, wa2_l, ba_l, tabs_p, tp // tq, tq, stack=(l, depth, k_stack, v_stack))
        b3 = lambda a: a.reshape(bp, tp, a.shape[-1])
        go, s_p = _gla(b3(gq), b3(gk), b3(gg), b3(gv), b3(gr), gnw, None, bp, gla_tt, GLA_CHUNK, GLA_SUB, None)
        do = _diff_prompt(diff_lambda[l], dnw, b3(dq), b3(dkb),
                          dvt.reshape(bp, tp // tq, DIFF_HEADS, DIFF_DV, tq), tq, lam_init)
        mk_p, mv_p = _memkv(mem2, wk_l, wv_l)
        x2 = _mix_cross(go, do, xp.reshape(bp, tp, D_MODEL), w_out_l, ln[0], ln[1], wq_l,
                        mk_p.reshape(1, bp, MEM_LEN * _MEM_SPLIT, LANES),
                        mv_p.reshape(1, bp, MEM_LEN * _MEM_SPLIT, LANES), 0,
                        wo_l, ln[2], ln[3], tm_p, alpha)
        x3, conv_p = _ffn(x2, zero_conv, w_up_l, cw_l, cb_l, w_down_l, ln[4], ln[5], tm_p, 1, tm_p, alpha)
        xp = x3.reshape(bp * tp, D_MODEL)
        outs[4].append(s_p)
        outs[6].append(conv_p[:, -1].transpose(0, 2, 1, 3, 4).reshape(bp, CONV_W - 1, 2 * D_FF))
        outs[8].append(_mem_rows_unview(mk_p.reshape(bp, MEM_LEN * _MEM_SPLIT, LANES)))
        outs[9].append(_mem_rows_unview(mv_p.reshape(bp, MEM_LEN * _MEM_SPLIT, LANES)))

        gq, gk, gg, gv, gr, dq, dkf, dvf, dkb, dvb, _ = _inproj(xs, w_in_l, wa2_l, ba_l, tabs_s, 1, bs * SAMPLE_PAD)
        s3 = lambda a: a.reshape(bs, SAMPLE_PAD, a.shape[-1])
        go, s_s = _gla(s3(gq), s3(gk), s3(gg), s3(gv), s3(gr), gnw, state_gla[l],
                       math.gcd(bs, 8), SAMPLE_PAD, SAMPLE_PAD, SAMPLE_PAD, ts)
        dq_s = s3(dq)[:, :_QROWS].astype(F32)
        do8 = _diff_sample(page_table, diff_lambda[l], dnw, dq_s, s3(dkb), s3(dvb),
                           cache_k4, cache_v4, l, pages, ts, lam_init)
        do = jnp.pad(do8, ((0, 0), (0, SAMPLE_PAD - _QROWS), (0, 0))).astype(BF16)
        x2 = _mix_cross_split(go, do, xs.reshape(bs, SAMPLE_PAD, D_MODEL), w_out_l, ln[0], ln[1], wq_l,
                              cmem_k, cmem_v, l, wo_l, ln[2], ln[3], alpha)
        x2t = x2.transpose(1, 0, 2).reshape(1, SAMPLE_PAD * bs, D_MODEL)
        x3t, conv_s = _ffn(x2t, _conv_state_in(state_conv[l]), w_up_l, cw_l, cb_l, w_down_l, ln[4], ln[5],
                           SAMPLE_PAD * bs, bs, ts, alpha)
        xs = x3t.reshape(SAMPLE_PAD, bs, D_MODEL).transpose(1, 0, 2).reshape(bs * SAMPLE_PAD, D_MODEL)
        outs[2].append(dkf.reshape(bs, SAMPLE_PAD, DIFF_HEADS, 2 * DIFF_DH)[:, :ts])
        outs[3].append(dvf.reshape(bs, SAMPLE_PAD, DIFF_HEADS, DIFF_DV)[:, :ts])
        outs[5].append(s_s)
        outs[7].append(_conv_state_out(conv_s[:, -1]))

    outs[0] = k_stack.reshape(depth, bp, tp // PAGE_SIZE, PAGE_SIZE, DIFF_HEADS, 2 * DIFF_DH)
    outs[1] = v_stack.reshape(depth, bp, tp // PAGE_SIZE, PAGE_SIZE, DIFF_HEADS, DIFF_DV)
    st = [o if i < 2 else jnp.stack(o) for i, o in enumerate(outs)]
    y_prompt = xp.reshape(bp, tp, D_MODEL)
    y_sample = xs.reshape(bs, SAMPLE_PAD, D_MODEL)[:, :ts]
    return (y_prompt, y_sample, st[0], st[1], st[2], st[3], st[4], st[5], st[6], st[7], st[8], st[9])
```

```python
import functools
import math

import jax
import jax.numpy as jnp
from jax import lax
from jax.experimental import pallas as pl
from jax.experimental.pallas import tpu as pltpu

F32 = jnp.float32
BF16 = jnp.bfloat16

D_MODEL = 1024
PAGE_SIZE = 128
GLA_HEADS = 4
GLA_DK = 64
GLA_DV = 128
GLA_RANK = 16
GLA_TAU = 16.0
GLA_CHUNK = 64
GLA_SUB = 16
DIFF_HEADS = 4
DIFF_DH = 64
DIFF_DV = 128
ROPE_THETA = 10000.0
MEM_LEN = 256
MEM_HEADS = 4
MEM_DH = D_MODEL // MEM_HEADS
D_FF = 2816
CONV_W = 3
LN_EPS = 1e-5
LOG2E = 1.4426950408889634

GLA_QK = GLA_HEADS * GLA_DK
GLA_V = GLA_HEADS * GLA_DV
DIFF_QK = DIFF_HEADS * 2 * DIFF_DH
DIFF_V = DIFF_HEADS * DIFF_DV
LANES = 128
N_IN_PAD = 2 * GLA_QK + 2 * GLA_V + 2 * DIFF_QK + DIFF_V + LANES
SAMPLE_PAD = 16
VMEM_LIMIT = 52 * 1024 * 1024


def _cparams(sem):
    return pltpu.CompilerParams(dimension_semantics=sem, vmem_limit_bytes=VMEM_LIMIT)


def _post_ln(res, sub, g, b, alpha):
    h = alpha * res + sub
    mu = jnp.mean(h, axis=-1, keepdims=True)
    hc = h - mu
    var = jnp.mean(hc * hc, axis=-1, keepdims=True)
    return hc * lax.rsqrt(var + LN_EPS) * g + b


def _dot(a, b):
    return jnp.dot(a, b, preferred_element_type=F32)


def _dot_nt(a, b):
    return lax.dot_general(a, b, (((1,), (1,)), ((), ())), preferred_element_type=F32)


def _dot_tn(a, b):
    return lax.dot_general(a, b, (((0,), (0,)), ((), ())), preferred_element_type=F32)


_O_GQ, _O_GK, _O_GV, _O_GR = 0, GLA_QK, 2 * GLA_QK, 2 * GLA_QK + GLA_V
_O_DQ = 2 * GLA_QK + 2 * GLA_V
_O_DK = _O_DQ + DIFF_QK
_O_DV = _O_DK + DIFF_QK
_O_GA = _O_DV + DIFF_V


def _inproj_kernel(x_ref, w_ref, wa2_ref, ba_ref, c_ref, sa_ref, sb_ref,
                   gq_ref, gk_ref, gg_ref, gv_ref, gr_ref, dq_ref, dkf_ref, dvf_ref, dkb_ref, dvb_ref, dvt_ref):
    h = _dot(x_ref[...].astype(BF16), w_ref[...])
    gq_ref[...] = h[:, _O_GQ:_O_GQ + GLA_QK] * (GLA_DK ** -0.5)
    gk_ref[...] = h[:, _O_GK:_O_GK + GLA_QK]
    gv_ref[...] = h[:, _O_GV:_O_GV + GLA_V].astype(BF16)
    gr_ref[...] = h[:, _O_GR:_O_GR + GLA_V]
    xg = _dot(h[:, _O_GA:_O_GA + LANES].astype(BF16), wa2_ref[...]) + ba_ref[...]
    gg_ref[...] = (jnp.minimum(xg, 0.0) - jnp.log1p(jnp.exp(-jnp.abs(xg)))) * (1.0 / GLA_TAU)
    c, sa, sb = c_ref[...], sa_ref[...], sb_ref[...]

    def rope(v):
        return v * c + pltpu.roll(v, LANES - DIFF_DH // 2, 1) * sa + pltpu.roll(v, DIFF_DH // 2, 1) * sb

    qscale = (DIFF_DH ** -0.5) * LOG2E
    tm = h.shape[0]
    for j in range(DIFF_HEADS):
        sl = slice(j * LANES, (j + 1) * LANES)
        head_rows = pl.ds(j, tm, stride=DIFF_HEADS)
        dq_ref[:, sl] = (rope(h[:, _O_DQ + j * LANES:_O_DQ + (j + 1) * LANES]) * qscale).astype(BF16)
        rk = rope(h[:, _O_DK + j * LANES:_O_DK + (j + 1) * LANES])
        dkf_ref[head_rows, :] = rk
        dkb_ref[:, sl] = rk.astype(BF16)
        dv = h[:, _O_DV + j * LANES:_O_DV + (j + 1) * LANES]
        dvf_ref[head_rows, :] = dv
        dvb_ref[:, sl] = dv.astype(BF16)
        dvt_ref[j] = dv.T.astype(BF16)


def _inproj(x, w, wa2, ba, tabs, tab_blocks, tm):
    m = x.shape[0]
    row = lambda n: pl.BlockSpec((tm, n), lambda i: (i, 0))
    full = lambda a: pl.BlockSpec(a.shape, lambda i: (0,) * a.ndim)
    tab = pl.BlockSpec((tm, LANES), lambda i: (i % tab_blocks, 0))
    shapes = [(1, GLA_QK, F32), (1, GLA_QK, F32), (1, GLA_QK, F32), (1, GLA_V, BF16), (1, GLA_V, F32),
              (1, DIFF_QK, BF16), (DIFF_HEADS, LANES, F32), (DIFF_HEADS, LANES, F32),
              (1, DIFF_QK, BF16), (1, DIFF_V, BF16)]
    out_specs = [pl.BlockSpec((tm * k, n), lambda i: (i, 0)) for k, n, _ in shapes]
    out_shape = [jax.ShapeDtypeStruct((m * k, n), dt) for k, n, dt in shapes]
    out_specs.append(pl.BlockSpec((None, DIFF_HEADS, DIFF_DV, tm), lambda i: (i, 0, 0, 0)))
    out_shape.append(jax.ShapeDtypeStruct((m // tm, DIFF_HEADS, DIFF_DV, tm), BF16))
    return pl.pallas_call(
        _inproj_kernel,
        grid=(m // tm,),
        in_specs=[row(D_MODEL), full(w), full(wa2), full(ba), tab, tab, tab],
        out_specs=out_specs,
        out_shape=out_shape,
        compiler_params=_cparams(("parallel",)),
        name="inproj",
    )(x, w, wa2, ba, *tabs)


def _bdot(a, b, contract):
    return lax.dot_general(a, b, (((contract[0],), (contract[1],)), ((0,), (0,))), preferred_element_type=F32)


def _gla_chunk(qc, kc, gc, vc, st, tril, head_a, st_mask, chunk, sub):
    n = qc.shape[0]
    nsub = chunk // sub
    g1 = gc.astype(BF16)
    r1 = gc - g1.astype(F32)
    g2 = r1.astype(BF16)
    g3 = (r1 - g2.astype(F32)).astype(BF16)
    b3 = _dot(tril, jnp.concatenate([g[i] for i in range(n) for g in (g1, g2, g3)], axis=1))
    b = jnp.stack([b3[:, (3 * i) * LANES:(3 * i + 1) * LANES] + b3[:, (3 * i + 1) * LANES:(3 * i + 2) * LANES]
                   + b3[:, (3 * i + 2) * LANES:(3 * i + 3) * LANES] for i in range(n)])
    b_last = b[:, chunk - 1:chunk, :]

    def by_head(x):
        return jnp.concatenate([jnp.where(head_a, x, 0.0), jnp.where(head_a, 0.0, x)], axis=1).astype(BF16)

    oi = _bdot(by_head(qc * jnp.exp(b)), st.astype(BF16), (2, 2))
    o_inter = jnp.concatenate([oi[:, 0:chunk, 0:GLA_DV], oi[:, chunk:, GLA_DV:]], axis=2)
    key_row = lax.broadcasted_iota(jnp.int32, (1, chunk, 1), 1)
    lqs, kss = [], []
    for s in range(nsub):
        r0, r1_ = s * sub, (s + 1) * sub
        anchor = b[:, r0:r0 + 1, :]
        lqs.append(by_head(qc[:, r0:r1_] * jnp.exp(b[:, r0:r1_] - anchor)))
        kss.append((kc * jnp.exp(jnp.where(key_row < r1_, anchor - b, 0.0))).astype(BF16))
    att = _bdot(jnp.concatenate(lqs, axis=0), jnp.concatenate(kss, axis=0), (2, 2))
    sub_idx = lax.broadcasted_iota(jnp.int32, att.shape, 0) // n
    q_tok = sub_idx * sub + lax.broadcasted_iota(jnp.int32, att.shape, 1) % sub
    att = jnp.where(lax.broadcasted_iota(jnp.int32, att.shape, 2) <= q_tok, att, 0.0).astype(BF16)
    att = jnp.concatenate([att[s * n:(s + 1) * n] for s in range(nsub)], axis=1)
    ov = _bdot(att, vc, (2, 1))
    o_intra = jnp.concatenate(
        [jnp.concatenate([ov[:, 2 * s * sub:(2 * s + 1) * sub, 0:GLA_DV],
                          ov[:, (2 * s + 1) * sub:(2 * s + 2) * sub, GLA_DV:]], axis=2) for s in range(nsub)], axis=1)
    upd = _bdot(vc, (kc * jnp.exp(b_last - b)).astype(BF16), (1, 1))
    return o_inter + o_intra, st * jnp.exp(b_last) + jnp.where(st_mask, upd, 0.0)


def _gla_kernel(*refs, bb, tt, chunk, sub, t_valid, has_s0):
    if has_s0:
        q_ref, k_ref, g_ref, v_ref, gr_ref, nw_ref, s0_ref, o_ref, s_out_ref, st_scr = refs
    else:
        q_ref, k_ref, g_ref, v_ref, gr_ref, nw_ref, o_ref, s_out_ref, st_scr = refs
    i = pl.program_id(1)
    pairs = GLA_HEADS // 2
    head_a = lax.broadcasted_iota(jnp.int32, (1, LANES), 1) < GLA_DK
    st_mask = (lax.broadcasted_iota(jnp.int32, (2 * GLA_DV, LANES), 0) // GLA_DV
               == lax.broadcasted_iota(jnp.int32, (2 * GLA_DV, LANES), 1) // GLA_DK)
    tril = (lax.broadcasted_iota(jnp.int32, (chunk, chunk), 0)
            >= lax.broadcasted_iota(jnp.int32, (chunk, chunk), 1)).astype(BF16)
    row_in_chunk = lax.broadcasted_iota(jnp.int32, (chunk, 1), 0)

    @pl.when(i == 0)
    def _():
        for bi in range(bb):
            for p in range(pairs):
                if has_s0:
                    s0t = s0_ref[bi, 2 * p:2 * p + 2].reshape(2 * GLA_DK, GLA_DV).T
                    st_scr[bi, p] = jnp.where(st_mask, jnp.concatenate([s0t, s0t], axis=0), 0.0)
                else:
                    st_scr[bi, p] = jnp.zeros((2 * GLA_DV, LANES), F32)

    nw = nw_ref[...]
    chains = [(bi, p) for bi in range(bb) for p in range(pairs)]
    ks = lambda p: slice(p * LANES, (p + 1) * LANES)
    vs = lambda p: slice(p * 2 * GLA_DV, (p + 1) * 2 * GLA_DV)
    st = st_scr[...].reshape(len(chains), 2 * GLA_DV, LANES)
    for c in range(tt // chunk):
        rs = slice(c * chunk, (c + 1) * chunk)
        qc = jnp.stack([q_ref[bi, rs, ks(p)] for bi, p in chains])
        kc = jnp.stack([k_ref[bi, rs, ks(p)] for bi, p in chains])
        gc = jnp.stack([g_ref[bi, rs, ks(p)] for bi, p in chains])
        vc = jnp.stack([v_ref[bi, rs, vs(p)] for bi, p in chains])
        if t_valid is not None:
            ok = (row_in_chunk + c * chunk) < t_valid
            kc = jnp.where(ok, kc, 0.0)
            gc = jnp.where(ok, gc, 0.0)
            vc = jnp.where(ok, vc, jnp.zeros_like(vc))
        o, st = _gla_chunk(qc, kc, gc, vc, st, tril, head_a, st_mask, chunk, sub)
        gr = jnp.stack([gr_ref[bi, rs, vs(p)] for bi, p in chains])
        normed = []
        for hh in range(2):
            oh = o[:, :, hh * GLA_DV:(hh + 1) * GLA_DV]
            normed.append(oh * lax.rsqrt(jnp.mean(oh * oh, axis=-1, keepdims=True) + LN_EPS) * nw)
        out = (jnp.concatenate(normed, axis=2) * (gr * (1.0 / (1.0 + jnp.exp(-gr))))).astype(o_ref.dtype)
        for idx, (bi, p) in enumerate(chains):
            o_ref[bi, rs, vs(p)] = out[idx]
    st_scr[...] = st.reshape(st_scr.shape)

    @pl.when(i == pl.num_programs(1) - 1)
    def _():
        for bi in range(bb):
            for p in range(pairs):
                st = st_scr[bi, p]
                sa = st[0:GLA_DV].T
                sb = st[GLA_DV:].T
                s_out_ref[bi, 2 * p:2 * p + 2] = jnp.concatenate(
                    [sa[0:GLA_DK], sb[GLA_DK:]], axis=0).reshape(2, GLA_DK, GLA_DV)


def _gla(q, k, g, v, gr, nw, s0, bb, tt, chunk, sub, t_valid):
    bsz, t, _ = q.shape
    has_s0 = s0 is not None
    qk_spec = pl.BlockSpec((bb, tt, GLA_QK), lambda b, i: (b, i, 0))
    v_spec = pl.BlockSpec((bb, tt, GLA_V), lambda b, i: (b, i, 0))
    s_spec = pl.BlockSpec((bb, GLA_HEADS, GLA_DK, GLA_DV), lambda b, i: (b, 0, 0, 0))
    in_specs = [qk_spec, qk_spec, qk_spec, v_spec, v_spec, pl.BlockSpec(nw.shape, lambda b, i: (0, 0))]
    args = [q, k, g, v, gr, nw]
    if has_s0:
        in_specs.append(s_spec)
        args.append(s0)
    return pl.pallas_call(
        functools.partial(_gla_kernel, bb=bb, tt=tt, chunk=chunk, sub=sub, t_valid=t_valid, has_s0=has_s0),
        grid=(bsz // bb, t // tt),
        in_specs=in_specs,
        out_specs=[v_spec, s_spec],
        out_shape=[jax.ShapeDtypeStruct((bsz, t, GLA_V), BF16),
                   jax.ShapeDtypeStruct((bsz, GLA_HEADS, GLA_DK, GLA_DV), F32)],
        scratch_shapes=[pltpu.VMEM((bb, GLA_HEADS // 2, 2 * GLA_DV, LANES), F32)],
        compiler_params=_cparams(("parallel", "arbitrary")),
        name="gla",
    )(*args)


def _diff_lambda(lam_ref, lam_init):
    lf = lam_ref[...]
    a = jnp.sum(lf[0:1] * lf[1:2], axis=-1, keepdims=True)
    b = jnp.sum(lf[2:3] * lf[3:4], axis=-1, keepdims=True)
    return jnp.exp(a) - jnp.exp(b) + lam_init


def _diff_finish(o0, o1, lam, nw, lam_init):
    od = o0 - lam * o1
    ms = jnp.mean(od * od, axis=-1, keepdims=True)
    return od * lax.rsqrt(ms + LN_EPS) * nw * (1.0 - lam_init)


_ONES_ROWS = 16


def _diff_prompt_kernel(lam_ref, nw_ref, q_ref, k_ref, vt_ref, o_ref,
                        qz_scr, s_scr, p_scr, m_scr, a_scr, acc_scr, *, tq, lam_init):
    qi = pl.program_id(2)
    qt = q_ref[...].astype(F32).T
    row = lax.broadcasted_iota(jnp.int32, (LANES, 1), 0)
    qz_scr[:, 0:tq] = jnp.where(row < DIFF_DH, qt, 0.0).astype(BF16)
    qz_scr[:, tq:] = jnp.where(row < DIFF_DH, 0.0, qt).astype(BF16)
    lane = lax.broadcasted_iota(jnp.int32, (1, LANES), 1)
    m_scr[...] = jnp.full_like(m_scr, -jnp.inf)
    acc_scr[...] = jnp.zeros_like(acc_scr)
    ones = jnp.ones((_ONES_ROWS, tq), BF16)
    key_row = lax.broadcasted_iota(jnp.int32, (tq, 1), 0)

    def scores(t, slot):
        ks = pl.multiple_of(t * tq, tq)
        s_scr[slot] = _dot(k_ref[pl.ds(ks, tq), :], qz_scr[...])

    def softmax(slot, masked):
        for c in range(2 * tq // LANES):
            cols = slice(c * LANES, (c + 1) * LANES)
            s = s_scr[slot, :, cols]
            if masked:
                s = jnp.where(key_row <= lane + (c * LANES) % tq, s, -jnp.inf)
            m_prev = m_scr[:, cols]
            m_new = jnp.maximum(m_prev, jnp.max(s, axis=0, keepdims=True))
            a_scr[slot, :, cols] = jnp.exp2(m_prev - m_new)
            m_scr[:, cols] = m_new
            p_scr[slot, :, cols] = jnp.exp2(s - m_new[0:1]).astype(BF16)

    def values(t, slot):
        v1t = jnp.concatenate([vt_ref[t], ones], axis=0)
        acc_scr[...] = acc_scr[...] * a_scr[slot, 0:1, :] + _dot(v1t, p_scr[slot])

    def step(t, slot):
        scores(t + 1, 1 - slot)
        softmax(slot, False)
        values(jnp.maximum(t - 1, 0), 1 - slot)

    def idle(slot):
        p_scr[slot] = jnp.zeros(p_scr.shape[1:], BF16)
        a_scr[slot] = jnp.ones(a_scr.shape[1:], F32)

    off = qi % 2

    @pl.when(off == 0)
    def _():
        idle(1)
        scores(0, 0)

    @pl.when(off == 1)
    def _():
        idle(0)
        scores(0, 1)
        step(0, 1)

    def body(u, carry):
        step(2 * u + off, 0)
        step(2 * u + 1 + off, 1)
        return carry

    lax.fori_loop(0, qi // 2, body, 0)
    softmax(0, True)
    values(jnp.maximum(qi - 1, 0), 1)
    values(qi, 0)

    acc = acc_scr[...]
    o = acc[0:DIFF_DV, :] / acc[DIFF_DV:DIFF_DV + 1, :]
    lam = _diff_lambda(lam_ref, lam_init)
    o_ref[...] = _diff_finish(o[:, 0:tq].T, o[:, tq:].T, lam, nw_ref[...], lam_init).astype(o_ref.dtype)


def _diff_prompt(lam_p, nw, q, k, vt, tq, lam_init):
    bsz, t, _ = q.shape
    qspec = pl.BlockSpec((None, tq, LANES), lambda b, h, i: (b, i, h))
    kspec = pl.BlockSpec((None, t, LANES), lambda b, h, i: (b, 0, h))
    vspec = pl.BlockSpec((None, t // tq, None, DIFF_DV, tq), lambda b, h, i: (b, 0, h, 0, 0))
    small = lambda a: pl.BlockSpec(a.shape, lambda b, h, i: (0,) * a.ndim)
    return pl.pallas_call(
        functools.partial(_diff_prompt_kernel, tq=tq, lam_init=lam_init),
        grid=(bsz, DIFF_HEADS, t // tq),
        in_specs=[small(lam_p), small(nw), qspec, kspec, vspec],
        out_specs=qspec,
        out_shape=jax.ShapeDtypeStruct((bsz, t, DIFF_V), BF16),
        scratch_shapes=[pltpu.VMEM((LANES, 2 * tq), BF16),
                        pltpu.VMEM((2, tq, 2 * tq), F32),
                        pltpu.VMEM((2, tq, 2 * tq), BF16),
                        pltpu.VMEM((8, 2 * tq), F32),
                        pltpu.VMEM((2, 8, 2 * tq), F32),
                        pltpu.VMEM((DIFF_DV + _ONES_ROWS, 2 * tq), F32)],
        compiler_params=_cparams(("parallel", "parallel", "arbitrary")),
        name="diff_prompt",
    )(lam_p, nw, q, k, vt)


_QROWS = 8


def _diff_sample_kernel(pt_ref, lam_ref, nw_ref, q_ref, kn_ref, vn_ref, *rest, pages, t_new, lam_init):
    k_refs, v_refs = rest[:pages], rest[pages:2 * pages]
    o_ref, qz_scr, m_scr, l_scr, acc_scr = rest[2 * pages:]
    j = pl.program_id(1)
    grp = 2 * _QROWS

    def update(keys, vals, mask):
        s = jnp.concatenate([_dot_nt(qz_scr[h * grp:(h + 1) * grp, :], keys[h]) for h in range(DIFF_HEADS)], axis=0)
        if mask is not None:
            s = jnp.where(mask, s, -jnp.inf)
        m_prev = m_scr[:, 0:1]
        m_new = jnp.maximum(m_prev, jnp.max(s, axis=1, keepdims=True))
        alpha = jnp.exp2(m_prev - m_new)
        p = jnp.exp2(s - m_new)
        l_new = l_scr[:, 0:1] * alpha + jnp.sum(p, axis=1, keepdims=True)
        pb = p.astype(BF16)
        pv = jnp.concatenate([_dot(pb[h * grp:(h + 1) * grp], vals[h]) for h in range(DIFF_HEADS)], axis=0)
        l_scr[...] = jnp.broadcast_to(l_new, l_scr.shape)
        acc_scr[...] = acc_scr[...] * alpha + pv
        m_scr[...] = jnp.broadcast_to(m_new, m_scr.shape)

    @pl.when(j == 0)
    def _():
        q = q_ref[...]
        lane = lax.broadcasted_iota(jnp.int32, (1, LANES), 1)
        m_scr[...] = jnp.full_like(m_scr, -jnp.inf)
        l_scr[...] = jnp.zeros_like(l_scr)
        acc_scr[...] = jnp.zeros_like(acc_scr)
        for h in range(DIFF_HEADS):
            qh = q[:, h * LANES:(h + 1) * LANES]
            qz_scr[h * grp:(h + 1) * grp, :] = jnp.concatenate(
                [jnp.where(lane < DIFF_DH, qh, 0.0), jnp.where(lane < DIFF_DH, 0.0, qh)], axis=0).astype(BF16)
        rr = lax.broadcasted_iota(jnp.int32, (DIFF_HEADS * grp, SAMPLE_PAD), 0) % _QROWS
        cc = lax.broadcasted_iota(jnp.int32, (DIFF_HEADS * grp, SAMPLE_PAD), 1)
        update([kn_ref[:, h * LANES:(h + 1) * LANES] for h in range(DIFF_HEADS)],
               [vn_ref[:, h * LANES:(h + 1) * LANES] for h in range(DIFF_HEADS)], (cc <= rr) & (cc < t_new))

    def head_of(refs, h):
        return jnp.concatenate([r[pl.ds(h, PAGE_SIZE, stride=DIFF_HEADS), :].astype(BF16) for r in refs], axis=0)

    update([head_of(k_refs, h) for h in range(DIFF_HEADS)], [head_of(v_refs, h) for h in range(DIFF_HEADS)], None)

    @pl.when(j == pl.num_programs(1) - 1)
    def _():
        acc = acc_scr[...]
        inv = 1.0 / l_scr[:, 0:1]
        lam = _diff_lambda(lam_ref, lam_init)
        nw = nw_ref[...]
        outs = []
        for h in range(DIFF_HEADS):
            r0 = h * grp
            r1 = r0 + _QROWS
            o0 = acc[r0:r0 + _QROWS] * inv[r0:r0 + _QROWS]
            o1 = acc[r1:r1 + _QROWS] * inv[r1:r1 + _QROWS]
            outs.append(_diff_finish(o0, o1, lam, nw, lam_init))
        o_ref[...] = jnp.concatenate(outs, axis=1)


def _diff_sample(page_table, lam_p, nw, q, k_new, v_new, cache_k, cache_v, layer, pages, t_new, lam_init):
    bsz, n_pages = page_table.shape
    small = lambda a: pl.BlockSpec(a.shape, lambda b, j, pt: (0,) * a.ndim)
    qspec = pl.BlockSpec((None, _QROWS, DIFF_QK), lambda b, j, pt: (b, 0, 0))
    nspec = pl.BlockSpec((None, SAMPLE_PAD, DIFF_QK), lambda b, j, pt: (b, 0, 0))

    def page_spec(i):
        return pl.BlockSpec((None, None, PAGE_SIZE * DIFF_HEADS, LANES),
                            lambda b, j, pt: (layer, pt[b, j * pages + i], 0, 0))

    nrow = 2 * DIFF_HEADS * _QROWS
    grid_spec = pltpu.PrefetchScalarGridSpec(
        num_scalar_prefetch=1,
        grid=(bsz, n_pages // pages),
        in_specs=[small(lam_p), small(nw), qspec, nspec, nspec]
                 + [page_spec(i) for i in range(pages)] + [page_spec(i) for i in range(pages)],
        out_specs=qspec,
        scratch_shapes=[pltpu.VMEM((nrow, LANES), BF16), pltpu.VMEM((nrow, LANES), F32),
                        pltpu.VMEM((nrow, LANES), F32), pltpu.VMEM((nrow, DIFF_DV), F32)],
    )
    return pl.pallas_call(
        functools.partial(_diff_sample_kernel, pages=pages, t_new=t_new, lam_init=lam_init),
        grid_spec=grid_spec,
        out_shape=jax.ShapeDtypeStruct((bsz, _QROWS, DIFF_V), F32),
        compiler_params=_cparams(("parallel", "arbitrary")),
        name="diff_sample",
    )(page_table, lam_p, nw, q, k_new, v_new, *([cache_k] * pages), *([cache_v] * pages))


_MEM_SPLIT = D_MODEL // LANES
_MEM_PIECES = MEM_DH // LANES


def _mem_row(head, piece):
    return piece * MEM_HEADS + head


def _mem_rows_view(a):
    lead = a.shape[:-3]
    n = len(lead)
    a = a.reshape(*lead, MEM_LEN, MEM_HEADS, _MEM_PIECES, LANES)
    a = a.transpose(*range(n), n, n + 2, n + 1, n + 3)
    return a.reshape(*lead, MEM_LEN * _MEM_SPLIT, LANES)


def _mem_rows_unview(a):
    lead = a.shape[:-2]
    n = len(lead)
    a = a.reshape(*lead, MEM_LEN, _MEM_PIECES, MEM_HEADS, LANES)
    a = a.transpose(*range(n), n, n + 2, n + 1, n + 3)
    return a.reshape(*lead, MEM_LEN, MEM_HEADS, MEM_DH)


def _memkv_kernel(x_ref, wk_ref, wv_ref, k_ref, v_ref):
    x = x_ref[...].astype(BF16)
    m = x.shape[0]
    for w_ref, o_ref in ((wk_ref, k_ref), (wv_ref, v_ref)):
        y = _dot(x, w_ref[...])
        for j in range(_MEM_SPLIT):
            o_ref[pl.ds(_mem_row(j // _MEM_PIECES, j % _MEM_PIECES), m, stride=_MEM_SPLIT), :] = (
                y[:, j * LANES:(j + 1) * LANES])


def _memkv(mem, wk, wv):
    m = mem.shape[0]
    full = lambda a: pl.BlockSpec(a.shape, lambda i: (0, 0))
    out = pl.BlockSpec((m * _MEM_SPLIT, LANES), lambda i: (0, 0))
    return pl.pallas_call(
        _memkv_kernel,
        grid=(1,),
        in_specs=[full(mem), full(wk), full(wv)],
        out_specs=[out, out],
        out_shape=[jax.ShapeDtypeStruct((m * _MEM_SPLIT, LANES), F32)] * 2,
        compiler_params=_cparams(("arbitrary",)),
        name="memkv",
    )(mem, wk, wv)


def _mix_proj(g_ref, d_ref, x_ref, wo1_ref, ln1g_ref, ln1b_ref, wq_ref, alpha):
    mix = _dot(g_ref[...], wo1_ref[0:GLA_V, :]) + _dot(d_ref[...], wo1_ref[GLA_V:, :])
    x1 = _post_ln(x_ref[...], mix, ln1g_ref[...], ln1b_ref[...], alpha)
    return x1, (_dot(x1.astype(BF16), wq_ref[...]) * ((MEM_DH ** -0.5) * LOG2E)).astype(BF16)


def _cross_heads(q, mk_ref, mv_ref):
    heads = []
    for h in range(MEM_HEADS):
        def head_of(ref):
            return jnp.concatenate([ref[pl.ds(_mem_row(h, j), MEM_LEN, stride=_MEM_SPLIT), :]
                                    for j in range(_MEM_PIECES)], axis=1).astype(BF16)

        s = _dot_nt(q[:, h * MEM_DH:(h + 1) * MEM_DH], head_of(mk_ref))
        p = jnp.exp2(s - jnp.max(s, axis=-1, keepdims=True))
        l = jnp.sum(p, axis=-1, keepdims=True)
        heads.append((_dot(p.astype(BF16), head_of(mv_ref)) / l).astype(BF16))
    return jnp.concatenate(heads, axis=1)


def _mix_cross_kernel(g_ref, d_ref, x_ref, wo1_ref, ln1g_ref, ln1b_ref, wq_ref, mk_ref, mv_ref, wo2_ref,
                      ln2g_ref, ln2b_ref, o_ref, *, alpha):
    x1, q = _mix_proj(g_ref, d_ref, x_ref, wo1_ref, ln1g_ref, ln1b_ref, wq_ref, alpha)
    y = _dot(_cross_heads(q, mk_ref, mv_ref), wo2_ref[...])
    o_ref[...] = _post_ln(x1, y, ln2g_ref[...], ln2b_ref[...], alpha)


def _mix_proj_kernel(g_ref, d_ref, x_ref, wo1_ref, ln1g_ref, ln1b_ref, wq_ref, x1_ref, q_ref, *, alpha):
    x1_ref[...], q_ref[...] = _mix_proj(g_ref, d_ref, x_ref, wo1_ref, ln1g_ref, ln1b_ref, wq_ref, alpha)


def _cross_kernel(q_ref, mk_ref, mv_ref, o_ref):
    o_ref[...] = _cross_heads(q_ref[...], mk_ref, mv_ref)


def _out_proj_kernel(o_ref, x1_ref, wo2_ref, ln2g_ref, ln2b_ref, y_ref, *, alpha):
    y_ref[...] = _post_ln(x1_ref[...], _dot(o_ref[...], wo2_ref[...]), ln2g_ref[...], ln2b_ref[...], alpha)


def _mix_cross_split(g, d, x, w_out, ln1g, ln1b, wq, mk, mv, mem_layer, wo, ln2g, ln2b, alpha):
    bsz, r, _ = x.shape
    rows = bsz * r
    flat = lambda a: a.reshape(rows, a.shape[-1])
    full = lambda a: pl.BlockSpec(a.shape, lambda *_: (0,) * a.ndim)
    x1, q = pl.pallas_call(
        functools.partial(_mix_proj_kernel, alpha=alpha),
        grid=(1,),
        in_specs=[full(flat(g)), full(flat(d)), full(flat(x)), full(w_out), full(ln1g), full(ln1b), full(wq)],
        out_specs=[full(flat(x)), full(flat(x))],
        out_shape=[jax.ShapeDtypeStruct((rows, D_MODEL), F32), jax.ShapeDtypeStruct((rows, D_MODEL), BF16)],
        compiler_params=_cparams(("arbitrary",)),
        name="mix_proj",
    )(flat(g), flat(d), flat(x), w_out, ln1g, ln1b, wq)
    seq = pl.BlockSpec((None, r, D_MODEL), lambda b: (b, 0, 0))
    mem = pl.BlockSpec((None, None, MEM_LEN * _MEM_SPLIT, LANES), lambda b: (mem_layer, b, 0, 0))
    o = pl.pallas_call(
        _cross_kernel,
        grid=(bsz,),
        in_specs=[seq, mem, mem],
        out_specs=seq,
        out_shape=jax.ShapeDtypeStruct((bsz, r, D_MODEL), BF16),
        compiler_params=_cparams(("parallel",)),
        name="cross",
    )(q.reshape(bsz, r, D_MODEL), mk, mv)
    y = pl.pallas_call(
        functools.partial(_out_proj_kernel, alpha=alpha),
        grid=(1,),
        in_specs=[full(flat(x)), full(flat(x)), full(wo), full(ln2g), full(ln2b)],
        out_specs=full(flat(x)),
        out_shape=jax.ShapeDtypeStruct((rows, D_MODEL), F32),
        compiler_params=_cparams(("arbitrary",)),
        name="out_proj",
    )(flat(o), x1, wo, ln2g, ln2b)
    return y.reshape(bsz, r, D_MODEL)


def _mix_cross(g, d, x, w_out, ln1g, ln1b, wq, mk, mv, mem_layer, wo, ln2g, ln2b, tm, alpha):
    bsz, r, _ = x.shape
    row = lambda n: pl.BlockSpec((None, tm, n), lambda b, i: (b, i, 0))
    full = lambda a: pl.BlockSpec(a.shape, lambda b, i: (0,) * a.ndim)
    mem = pl.BlockSpec((None, None, MEM_LEN * _MEM_SPLIT, LANES), lambda b, i: (mem_layer, b, 0, 0))
    return pl.pallas_call(
        functools.partial(_mix_cross_kernel, alpha=alpha),
        grid=(bsz, r // tm),
        in_specs=[row(GLA_V), row(DIFF_V), row(D_MODEL), full(w_out), full(ln1g), full(ln1b), full(wq),
                  mem, mem, full(wo), full(ln2g), full(ln2b)],
        out_specs=row(D_MODEL),
        out_shape=jax.ShapeDtypeStruct(x.shape, F32),
        compiler_params=_cparams(("parallel", "parallel")),
        name="mix_cross",
    )(g, d, x, w_out, ln1g, ln1b, wq, mk, mv, wo, ln2g, ln2b)


_CARRY = 8
_FF_CHUNK = 256


def _ffn_kernel(x_ref, st_ref, wua_ref, wub_ref, cw_ref, cb_ref, wd_ref, g_ref, b_ref,
                o_ref, ns_ref, carry_scr, h_scr, *, tm, stride, n_valid, alpha):
    i = pl.program_id(1)
    x = x_ref[...]
    xb = x.astype(BF16)
    rows8 = lax.broadcasted_iota(jnp.int32, (_CARRY, 1), 0)

    if stride == 1:
        @pl.when(i == 0)
        def _():
            for half in range(2):
                carry_scr[half, _CARRY - 2:_CARRY - 1, :] = st_ref[half, 0]
                carry_scr[half, _CARRY - 1:_CARRY, :] = st_ref[half, 1]

    for c in range(D_FF // _FF_CHUNK):
        cs = slice(c * _FF_CHUNK, (c + 1) * _FF_CHUNK)
        halves = []
        for half, w_ref in enumerate((wua_ref, wub_ref)):
            u = _dot(xb, w_ref[:, cs])
            if stride == 1:
                p1 = carry_scr[half, _CARRY - 1:_CARRY, cs]
                p2 = carry_scr[half, _CARRY - 2:_CARRY - 1, cs]
                r1, r2 = pltpu.roll(u, 1, 0), pltpu.roll(u, 2, 0)
                u1 = jnp.concatenate([jnp.where(rows8 == 0, p1, r1[0:_CARRY]), r1[_CARRY:]], axis=0)
                u2 = jnp.concatenate(
                    [jnp.where(rows8 == 0, p2, jnp.where(rows8 == 1, p1, r2[0:_CARRY])), r2[_CARRY:]], axis=0)
                carry_scr[half, :, cs] = u[tm - _CARRY:tm, :]
            else:
                padded = jnp.concatenate([st_ref[half, 0, :, cs], st_ref[half, 1, :, cs], u], axis=0)
                u1 = padded[stride:stride + tm]
                u2 = padded[0:tm]
            cw = cw_ref[half, :, cs]
            halves.append(cb_ref[half, :, cs] + cw[0:1] * u2 + cw[1:2] * u1 + cw[2:3] * u)
            for j in range(CONV_W - 1):
                r0 = (n_valid - (CONV_W - 1) + j) * stride
                ns_ref[half, j, :, cs] = u[r0:r0 + stride, :]
        a, bb = halves
        h_scr[:, cs] = (a * (1.0 / (1.0 + jnp.exp(-a))) * bb).astype(BF16)
    o_ref[...] = _post_ln(x, _dot(h_scr[...], wd_ref[...]), g_ref[...], b_ref[...], alpha)


def _ffn(x, state, w_up, conv_w, conv_b, w_down, g, b, tm, stride, n_valid, alpha):
    bsz, r, _ = x.shape
    assert stride == 1 or r == tm
    row = pl.BlockSpec((None, tm, D_MODEL), lambda bi, i: (bi, i, 0))
    st = pl.BlockSpec((None, 2, CONV_W - 1, stride, D_FF), lambda bi, i: (bi, 0, 0, 0, 0))
    once = pl.Buffered(1)
    full = lambda a: pl.BlockSpec(a.shape, lambda bi, i: (0,) * a.ndim, pipeline_mode=once)
    return pl.pallas_call(
        functools.partial(_ffn_kernel, tm=tm, stride=stride, n_valid=n_valid, alpha=alpha),
        grid=(bsz, r // tm),
        in_specs=[row, st,
                  pl.BlockSpec((D_MODEL, D_FF), lambda bi, i: (0, 0), pipeline_mode=once),
                  pl.BlockSpec((D_MODEL, D_FF), lambda bi, i: (0, 1), pipeline_mode=once),
                  full(conv_w), full(conv_b), full(w_down), full(g), full(b)],
        out_specs=[row, pl.BlockSpec((None, None, 2, CONV_W - 1, stride, D_FF), lambda bi, i: (bi, i, 0, 0, 0, 0))],
        out_shape=[jax.ShapeDtypeStruct(x.shape, F32),
                   jax.ShapeDtypeStruct((bsz, r // tm, 2, CONV_W - 1, stride, D_FF), F32)],
        scratch_shapes=[pltpu.VMEM((2, _CARRY, D_FF), F32), pltpu.VMEM((tm, D_FF), BF16)],
        compiler_params=_cparams(("parallel", "arbitrary")),
        name="ffn",
    )(x, state, w_up, w_up, conv_w, conv_b, w_down, g, b)


def _rope_tables(pos):
    half = DIFF_DH // 2
    inv = ROPE_THETA ** (-jnp.arange(half, dtype=F32) / half)
    ang = pos.astype(F32)[:, None] * inv[None, :]
    cos = jnp.tile(jnp.cos(ang), (1, LANES // half))
    sin = jnp.tile(jnp.sin(ang), (1, LANES // half))
    first = (jnp.arange(LANES) % DIFF_DH) < half
    return cos, jnp.where(first, -sin, 0.0), jnp.where(first, 0.0, sin)


def _conv_state_in(s):
    return s.reshape(s.shape[0], CONV_W - 1, 2, D_FF).transpose(2, 1, 0, 3)[None]


def _conv_state_out(s):
    return s[0].transpose(2, 1, 0, 3).reshape(s.shape[3], CONV_W - 1, 2 * D_FF)


def kernel(x_prompt, x_sample, mem_prompt, cache_k, cache_v, page_table, cache_mem_k, cache_mem_v,
           state_gla, state_conv, w_in, gla_wa2, gla_ba, gla_norm_w, diff_lambda, diff_norm_w, w_out,
           ln1_g, ln1_b, cross_wq, cross_wk, cross_wv, cross_wo, ln2_g, ln2_b,
           ffn_w_up, ffn_conv_w, ffn_conv_b, ffn_w_down, ln3_g, ln3_b):
    depth = w_in.shape[0]
    bp, tp, _ = x_prompt.shape
    bs, ts, _ = x_sample.shape
    n_pages = page_table.shape[1]
    past_len = n_pages * PAGE_SIZE
    alpha = (2 * depth) ** 0.25
    assert ts <= _QROWS and tp % 512 == 0

    tm_p = 512
    tq = 512
    gla_tt = 512
    pages = math.gcd(n_pages, 16)

    tabs_p = _rope_tables(jnp.arange(tp))
    pos_s = past_len + jnp.minimum(jnp.arange(SAMPLE_PAD), ts - 1)
    tabs_s = _rope_tables(jnp.tile(pos_s, bs))

    xp = x_prompt.reshape(bp * tp, D_MODEL)
    xs = jnp.pad(x_sample, ((0, 0), (0, SAMPLE_PAD - ts), (0, 0))).reshape(bs * SAMPLE_PAD, D_MODEL)
    mem2 = mem_prompt.reshape(bp * MEM_LEN, D_MODEL)
    zero_conv = jnp.zeros((bp, 2, CONV_W - 1, 1, D_FF), F32)
    n_phys = cache_k.shape[1]
    cache_k4 = cache_k.reshape(depth, n_phys, PAGE_SIZE * DIFF_HEADS, 2 * DIFF_DH)
    cache_v4 = cache_v.reshape(depth, n_phys, PAGE_SIZE * DIFF_HEADS, DIFF_DV)
    cmem_k = _mem_rows_view(cache_mem_k)
    cmem_v = _mem_rows_view(cache_mem_v)

    outs = [[] for _ in range(10)]
    for l in range(depth):
        lam_init = 0.8 - 0.6 * math.exp(-0.3 * l)
        wl = w_in[l]
        ga0 = 2 * GLA_QK + 2 * GLA_V
        w_in_l = jnp.concatenate(
            [wl[:, :ga0], wl[:, ga0 + GLA_RANK:], wl[:, ga0:ga0 + GLA_RANK],
             jnp.zeros((D_MODEL, LANES - GLA_RANK), F32)], axis=1).astype(BF16)
        wa2_l = jnp.pad(gla_wa2[l], ((0, LANES - GLA_RANK), (0, 0))).astype(BF16)
        ba_l = gla_ba[l].reshape(1, GLA_QK)
        gnw = gla_norm_w[l].reshape(1, GLA_DV)
        dnw = diff_norm_w[l].reshape(1, DIFF_DV)
        w_out_l = w_out[l].astype(BF16)
        wq_l = cross_wq[l].astype(BF16)
        wk_l = cross_wk[l].astype(BF16)
        wv_l = cross_wv[l].astype(BF16)
        wo_l = cross_wo[l].astype(BF16)
        w_up_l = ffn_w_up[l].astype(BF16)
        w_down_l = ffn_w_down[l].astype(BF16)
        cw_l = ffn_conv_w[l].reshape(CONV_W, 2, D_FF).transpose(1, 0, 2)
        cb_l = ffn_conv_b[l].reshape(2, 1, D_FF)
        r1 = lambda a: a[l].reshape(1, D_MODEL)
        ln = [r1(a) for a in (ln1_g, ln1_b, ln2_g, ln2_b, ln3_g, ln3_b)]

        gq, gk, gg, gv, gr, dq, dkf, dvf, dkb, _, dvt = _inproj(xp, w_in_l, wa2_l, ba_l, tabs_p, tp // tq, tq)
        b3 = lambda a: a.reshape(bp, tp, a.shape[-1])
        go, s_p = _gla(b3(gq), b3(gk), b3(gg), b3(gv), b3(gr), gnw, None, bp, gla_tt, GLA_CHUNK, GLA_SUB, None)
        do = _diff_prompt(diff_lambda[l], dnw, b3(dq), b3(dkb),
                          dvt.reshape(bp, tp // tq, DIFF_HEADS, DIFF_DV, tq), tq, lam_init)
        mk_p, mv_p = _memkv(mem2, wk_l, wv_l)
        x2 = _mix_cross(go, do, xp.reshape(bp, tp, D_MODEL), w_out_l, ln[0], ln[1], wq_l,
                        mk_p.reshape(1, bp, MEM_LEN * _MEM_SPLIT, LANES),
                        mv_p.reshape(1, bp, MEM_LEN * _MEM_SPLIT, LANES), 0,
                        wo_l, ln[2], ln[3], tm_p, alpha)
        x3, conv_p = _ffn(x2, zero_conv, w_up_l, cw_l, cb_l, w_down_l, ln[4], ln[5], tm_p, 1, tm_p, alpha)
        xp = x3.reshape(bp * tp, D_MODEL)
        outs[0].append(dkf.reshape(bp, tp // PAGE_SIZE, PAGE_SIZE, DIFF_HEADS, 2 * DIFF_DH))
        outs[1].append(dvf.reshape(bp, tp // PAGE_SIZE, PAGE_SIZE, DIFF_HEADS, DIFF_DV))
        outs[4].append(s_p)
        outs[6].append(conv_p[:, -1].transpose(0, 2, 1, 3, 4).reshape(bp, CONV_W - 1, 2 * D_FF))
        outs[8].append(_mem_rows_unview(mk_p.reshape(bp, MEM_LEN * _MEM_SPLIT, LANES)))
        outs[9].append(_mem_rows_unview(mv_p.reshape(bp, MEM_LEN * _MEM_SPLIT, LANES)))

        gq, gk, gg, gv, gr, dq, dkf, dvf, dkb, dvb, _ = _inproj(xs, w_in_l, wa2_l, ba_l, tabs_s, 1, bs * SAMPLE_PAD)
        s3 = lambda a: a.reshape(bs, SAMPLE_PAD, a.shape[-1])
        go, s_s = _gla(s3(gq), s3(gk), s3(gg), s3(gv), s3(gr), gnw, state_gla[l],
                       math.gcd(bs, 8), SAMPLE_PAD, SAMPLE_PAD, SAMPLE_PAD, ts)
        dq_s = s3(dq)[:, :_QROWS].astype(F32)
        do8 = _diff_sample(page_table, diff_lambda[l], dnw, dq_s, s3(dkb), s3(dvb),
                           cache_k4, cache_v4, l, pages, ts, lam_init)
        do = jnp.pad(do8, ((0, 0), (0, SAMPLE_PAD - _QROWS), (0, 0))).astype(BF16)
        x2 = _mix_cross_split(go, do, xs.reshape(bs, SAMPLE_PAD, D_MODEL), w_out_l, ln[0], ln[1], wq_l,
                              cmem_k, cmem_v, l, wo_l, ln[2], ln[3], alpha)
        x2t = x2.transpose(1, 0, 2).reshape(1, SAMPLE_PAD * bs, D_MODEL)
        x3t, conv_s = _ffn(x2t, _conv_state_in(state_conv[l]), w_up_l, cw_l, cb_l, w_down_l, ln[4], ln[5],
                           SAMPLE_PAD * bs, bs, ts, alpha)
        xs = x3t.reshape(SAMPLE_PAD, bs, D_MODEL).transpose(1, 0, 2).reshape(bs * SAMPLE_PAD, D_MODEL)
        outs[2].append(dkf.reshape(bs, SAMPLE_PAD, DIFF_HEADS, 2 * DIFF_DH)[:, :ts])
        outs[3].append(dvf.reshape(bs, SAMPLE_PAD, DIFF_HEADS, DIFF_DV)[:, :ts])
        outs[5].append(s_s)
        outs[7].append(_conv_state_out(conv_s[:, -1]))

    st = [jnp.stack(o) for o in outs]
    y_prompt = xp.reshape(bp, tp, D_MODEL)
    y_sample = xs.reshape(bs, SAMPLE_PAD, D_MODEL)[:, :ts]
    return (y_prompt, y_sample, st[0], st[1], st[2], st[3], st[4], st[5], st[6], st[7], st[8], st[9])
```

```python
import functools
import math

import jax
import jax.numpy as jnp
from jax import lax
from jax.experimental import pallas as pl
from jax.experimental.pallas import tpu as pltpu

F32 = jnp.float32
BF16 = jnp.bfloat16

D_MODEL = 1024
PAGE_SIZE = 128
GLA_HEADS = 4
GLA_DK = 64
GLA_DV = 128
GLA_RANK = 16
GLA_TAU = 16.0
GLA_CHUNK = 64
GLA_SUB = 16
DIFF_HEADS = 4
DIFF_DH = 64
DIFF_DV = 128
ROPE_THETA = 10000.0
MEM_LEN = 256
MEM_HEADS = 4
MEM_DH = D_MODEL // MEM_HEADS
D_FF = 2816
CONV_W = 3
LN_EPS = 1e-5
LOG2E = 1.4426950408889634

GLA_QK = GLA_HEADS * GLA_DK
GLA_V = GLA_HEADS * GLA_DV
DIFF_QK = DIFF_HEADS * 2 * DIFF_DH
DIFF_V = DIFF_HEADS * DIFF_DV
LANES = 128
N_IN_PAD = 2 * GLA_QK + 2 * GLA_V + 2 * DIFF_QK + DIFF_V + LANES
SAMPLE_PAD = 16
VMEM_LIMIT = 52 * 1024 * 1024


def _cparams(sem):
    return pltpu.CompilerParams(dimension_semantics=sem, vmem_limit_bytes=VMEM_LIMIT)


def _post_ln(res, sub, g, b, alpha):
    h = alpha * res + sub
    mu = jnp.mean(h, axis=-1, keepdims=True)
    hc = h - mu
    var = jnp.mean(hc * hc, axis=-1, keepdims=True)
    return hc * lax.rsqrt(var + LN_EPS) * g + b


def _dot(a, b):
    return jnp.dot(a, b, preferred_element_type=F32)


def _dot_nt(a, b):
    return lax.dot_general(a, b, (((1,), (1,)), ((), ())), preferred_element_type=F32)


def _dot_tn(a, b):
    return lax.dot_general(a, b, (((0,), (0,)), ((), ())), preferred_element_type=F32)


_O_GQ, _O_GK, _O_GV, _O_GR = 0, GLA_QK, 2 * GLA_QK, 2 * GLA_QK + GLA_V
_O_DQ = 2 * GLA_QK + 2 * GLA_V
_O_DK = _O_DQ + DIFF_QK
_O_DV = _O_DK + DIFF_QK
_O_GA = _O_DV + DIFF_V


def _inproj_kernel(x_ref, w_ref, wa2_ref, ba_ref, c_ref, sa_ref, sb_ref,
                   gq_ref, gk_ref, gg_ref, gv_ref, gr_ref, dq_ref, dkf_ref, dvf_ref, dkb_ref, dvb_ref, dvt_ref):
    h = _dot(x_ref[...].astype(BF16), w_ref[...])
    gq_ref[...] = h[:, _O_GQ:_O_GQ + GLA_QK] * (GLA_DK ** -0.5)
    gk_ref[...] = h[:, _O_GK:_O_GK + GLA_QK]
    gv_ref[...] = h[:, _O_GV:_O_GV + GLA_V].astype(BF16)
    gr_ref[...] = h[:, _O_GR:_O_GR + GLA_V]
    xg = _dot(h[:, _O_GA:_O_GA + LANES].astype(BF16), wa2_ref[...]) + ba_ref[...]
    gg_ref[...] = (jnp.minimum(xg, 0.0) - jnp.log1p(jnp.exp(-jnp.abs(xg)))) * (1.0 / GLA_TAU)
    c, sa, sb = c_ref[...], sa_ref[...], sb_ref[...]

    def rope(v):
        return v * c + pltpu.roll(v, LANES - DIFF_DH // 2, 1) * sa + pltpu.roll(v, DIFF_DH // 2, 1) * sb

    qscale = (DIFF_DH ** -0.5) * LOG2E
    tm = h.shape[0]
    for j in range(DIFF_HEADS):
        sl = slice(j * LANES, (j + 1) * LANES)
        head_rows = pl.ds(j, tm, stride=DIFF_HEADS)
        dq_ref[:, sl] = (rope(h[:, _O_DQ + j * LANES:_O_DQ + (j + 1) * LANES]) * qscale).astype(BF16)
        rk = rope(h[:, _O_DK + j * LANES:_O_DK + (j + 1) * LANES])
        dkf_ref[head_rows, :] = rk
        dkb_ref[:, sl] = rk.astype(BF16)
        dv = h[:, _O_DV + j * LANES:_O_DV + (j + 1) * LANES]
        dvf_ref[head_rows, :] = dv
        dvb_ref[:, sl] = dv.astype(BF16)
        dvt_ref[j] = dv.T.astype(BF16)


def _inproj(x, w, wa2, ba, tabs, tab_blocks, tm):
    m = x.shape[0]
    row = lambda n: pl.BlockSpec((tm, n), lambda i: (i, 0))
    full = lambda a: pl.BlockSpec(a.shape, lambda i: (0,) * a.ndim)
    tab = pl.BlockSpec((tm, LANES), lambda i: (i % tab_blocks, 0))
    shapes = [(1, GLA_QK, F32), (1, GLA_QK, F32), (1, GLA_QK, F32), (1, GLA_V, BF16), (1, GLA_V, F32),
              (1, DIFF_QK, BF16), (DIFF_HEADS, LANES, F32), (DIFF_HEADS, LANES, F32),
              (1, DIFF_QK, BF16), (1, DIFF_V, BF16)]
    out_specs = [pl.BlockSpec((tm * k, n), lambda i: (i, 0)) for k, n, _ in shapes]
    out_shape = [jax.ShapeDtypeStruct((m * k, n), dt) for k, n, dt in shapes]
    out_specs.append(pl.BlockSpec((None, DIFF_HEADS, DIFF_DV, tm), lambda i: (i, 0, 0, 0)))
    out_shape.append(jax.ShapeDtypeStruct((m // tm, DIFF_HEADS, DIFF_DV, tm), BF16))
    return pl.pallas_call(
        _inproj_kernel,
        grid=(m // tm,),
        in_specs=[row(D_MODEL), full(w), full(wa2), full(ba), tab, tab, tab],
        out_specs=out_specs,
        out_shape=out_shape,
        compiler_params=_cparams(("parallel",)),
        name="inproj",
    )(x, w, wa2, ba, *tabs)


def _bdot(a, b, contract):
    return lax.dot_general(a, b, (((contract[0],), (contract[1],)), ((0,), (0,))), preferred_element_type=F32)


def _gla_chunk(qc, kc, gc, vc, st, tril, head_a, st_mask, chunk, sub):
    n = qc.shape[0]
    nsub = chunk // sub
    g1 = gc.astype(BF16)
    r1 = gc - g1.astype(F32)
    g2 = r1.astype(BF16)
    g3 = (r1 - g2.astype(F32)).astype(BF16)
    b3 = _dot(tril, jnp.concatenate([g[i] for i in range(n) for g in (g1, g2, g3)], axis=1))
    b = jnp.stack([b3[:, (3 * i) * LANES:(3 * i + 1) * LANES] + b3[:, (3 * i + 1) * LANES:(3 * i + 2) * LANES]
                   + b3[:, (3 * i + 2) * LANES:(3 * i + 3) * LANES] for i in range(n)])
    b_last = b[:, chunk - 1:chunk, :]

    def by_head(x):
        return jnp.concatenate([jnp.where(head_a, x, 0.0), jnp.where(head_a, 0.0, x)], axis=1).astype(BF16)

    oi = _bdot(by_head(qc * jnp.exp(b)), st.astype(BF16), (2, 2))
    o_inter = jnp.concatenate([oi[:, 0:chunk, 0:GLA_DV], oi[:, chunk:, GLA_DV:]], axis=2)
    key_row = lax.broadcasted_iota(jnp.int32, (1, chunk, 1), 1)
    lqs, kss = [], []
    for s in range(nsub):
        r0, r1_ = s * sub, (s + 1) * sub
        anchor = b[:, r0:r0 + 1, :]
        lqs.append(by_head(qc[:, r0:r1_] * jnp.exp(b[:, r0:r1_] - anchor)))
        kss.append((kc * jnp.exp(jnp.where(key_row < r1_, anchor - b, 0.0))).astype(BF16))
    att = _bdot(jnp.concatenate(lqs, axis=0), jnp.concatenate(kss, axis=0), (2, 2))
    sub_idx = lax.broadcasted_iota(jnp.int32, att.shape, 0) // n
    q_tok = sub_idx * sub + lax.broadcasted_iota(jnp.int32, att.shape, 1) % sub
    att = jnp.where(lax.broadcasted_iota(jnp.int32, att.shape, 2) <= q_tok, att, 0.0).astype(BF16)
    att = jnp.concatenate([att[s * n:(s + 1) * n] for s in range(nsub)], axis=1)
    ov = _bdot(att, vc, (2, 1))
    o_intra = jnp.concatenate(
        [jnp.concatenate([ov[:, 2 * s * sub:(2 * s + 1) * sub, 0:GLA_DV],
                          ov[:, (2 * s + 1) * sub:(2 * s + 2) * sub, GLA_DV:]], axis=2) for s in range(nsub)], axis=1)
    upd = _bdot(vc, (kc * jnp.exp(b_last - b)).astype(BF16), (1, 1))
    return o_inter + o_intra, st * jnp.exp(b_last) + jnp.where(st_mask, upd, 0.0)


def _gla_kernel(*refs, bb, tt, chunk, sub, t_valid, has_s0):
    if has_s0:
        q_ref, k_ref, g_ref, v_ref, gr_ref, nw_ref, s0_ref, o_ref, s_out_ref, st_scr = refs
    else:
        q_ref, k_ref, g_ref, v_ref, gr_ref, nw_ref, o_ref, s_out_ref, st_scr = refs
    i = pl.program_id(1)
    pairs = GLA_HEADS // 2
    head_a = lax.broadcasted_iota(jnp.int32, (1, LANES), 1) < GLA_DK
    st_mask = (lax.broadcasted_iota(jnp.int32, (2 * GLA_DV, LANES), 0) // GLA_DV
               == lax.broadcasted_iota(jnp.int32, (2 * GLA_DV, LANES), 1) // GLA_DK)
    tril = (lax.broadcasted_iota(jnp.int32, (chunk, chunk), 0)
            >= lax.broadcasted_iota(jnp.int32, (chunk, chunk), 1)).astype(BF16)
    row_in_chunk = lax.broadcasted_iota(jnp.int32, (chunk, 1), 0)

    @pl.when(i == 0)
    def _():
        for bi in range(bb):
            for p in range(pairs):
                if has_s0:
                    s0t = s0_ref[bi, 2 * p:2 * p + 2].reshape(2 * GLA_DK, GLA_DV).T
                    st_scr[bi, p] = jnp.where(st_mask, jnp.concatenate([s0t, s0t], axis=0), 0.0)
                else:
                    st_scr[bi, p] = jnp.zeros((2 * GLA_DV, LANES), F32)

    nw = nw_ref[...]
    chains = [(bi, p) for bi in range(bb) for p in range(pairs)]
    ks = lambda p: slice(p * LANES, (p + 1) * LANES)
    vs = lambda p: slice(p * 2 * GLA_DV, (p + 1) * 2 * GLA_DV)
    st = st_scr[...].reshape(len(chains), 2 * GLA_DV, LANES)
    for c in range(tt // chunk):
        rs = slice(c * chunk, (c + 1) * chunk)
        qc = jnp.stack([q_ref[bi, rs, ks(p)] for bi, p in chains])
        kc = jnp.stack([k_ref[bi, rs, ks(p)] for bi, p in chains])
        gc = jnp.stack([g_ref[bi, rs, ks(p)] for bi, p in chains])
        vc = jnp.stack([v_ref[bi, rs, vs(p)] for bi, p in chains])
        if t_valid is not None:
            ok = (row_in_chunk + c * chunk) < t_valid
            kc = jnp.where(ok, kc, 0.0)
            gc = jnp.where(ok, gc, 0.0)
            vc = jnp.where(ok, vc, jnp.zeros_like(vc))
        o, st = _gla_chunk(qc, kc, gc, vc, st, tril, head_a, st_mask, chunk, sub)
        gr = jnp.stack([gr_ref[bi, rs, vs(p)] for bi, p in chains])
        normed = []
        for hh in range(2):
            oh = o[:, :, hh * GLA_DV:(hh + 1) * GLA_DV]
            normed.append(oh * lax.rsqrt(jnp.mean(oh * oh, axis=-1, keepdims=True) + LN_EPS) * nw)
        out = (jnp.concatenate(normed, axis=2) * (gr * (1.0 / (1.0 + jnp.exp(-gr))))).astype(o_ref.dtype)
        for idx, (bi, p) in enumerate(chains):
            o_ref[bi, rs, vs(p)] = out[idx]
    st_scr[...] = st.reshape(st_scr.shape)

    @pl.when(i == pl.num_programs(1) - 1)
    def _():
        for bi in range(bb):
            for p in range(pairs):
                st = st_scr[bi, p]
                sa = st[0:GLA_DV].T
                sb = st[GLA_DV:].T
                s_out_ref[bi, 2 * p:2 * p + 2] = jnp.concatenate(
                    [sa[0:GLA_DK], sb[GLA_DK:]], axis=0).reshape(2, GLA_DK, GLA_DV)


def _gla(q, k, g, v, gr, nw, s0, bb, tt, chunk, sub, t_valid):
    bsz, t, _ = q.shape
    has_s0 = s0 is not None
    qk_spec = pl.BlockSpec((bb, tt, GLA_QK), lambda b, i: (b, i, 0))
    v_spec = pl.BlockSpec((bb, tt, GLA_V), lambda b, i: (b, i, 0))
    s_spec = pl.BlockSpec((bb, GLA_HEADS, GLA_DK, GLA_DV), lambda b, i: (b, 0, 0, 0))
    in_specs = [qk_spec, qk_spec, qk_spec, v_spec, v_spec, pl.BlockSpec(nw.shape, lambda b, i: (0, 0))]
    args = [q, k, g, v, gr, nw]
    if has_s0:
        in_specs.append(s_spec)
        args.append(s0)
    return pl.pallas_call(
        functools.partial(_gla_kernel, bb=bb, tt=tt, chunk=chunk, sub=sub, t_valid=t_valid, has_s0=has_s0),
        grid=(bsz // bb, t // tt),
        in_specs=in_specs,
        out_specs=[v_spec, s_spec],
        out_shape=[jax.ShapeDtypeStruct((bsz, t, GLA_V), BF16),
                   jax.ShapeDtypeStruct((bsz, GLA_HEADS, GLA_DK, GLA_DV), F32)],
        scratch_shapes=[pltpu.VMEM((bb, GLA_HEADS // 2, 2 * GLA_DV, LANES), F32)],
        compiler_params=_cparams(("parallel", "arbitrary")),
        name="gla",
    )(*args)


def _diff_lambda(lam_ref, lam_init):
    lf = lam_ref[...]
    a = jnp.sum(lf[0:1] * lf[1:2], axis=-1, keepdims=True)
    b = jnp.sum(lf[2:3] * lf[3:4], axis=-1, keepdims=True)
    return jnp.exp(a) - jnp.exp(b) + lam_init


def _diff_finish(o0, o1, lam, nw, lam_init):
    od = o0 - lam * o1
    ms = jnp.mean(od * od, axis=-1, keepdims=True)
    return od * lax.rsqrt(ms + LN_EPS) * nw * (1.0 - lam_init)


_ONES_ROWS = 16


def _diff_prompt_kernel(lam_ref, nw_ref, q_ref, k_ref, vt_ref, o_ref,
                        qz_scr, s_scr, p_scr, m_scr, a_scr, acc_scr, *, tq, lam_init):
    qi = pl.program_id(2)
    qt = q_ref[...].astype(F32).T
    row = lax.broadcasted_iota(jnp.int32, (LANES, 1), 0)
    qz_scr[:, 0:tq] = jnp.where(row < DIFF_DH, qt, 0.0).astype(BF16)
    qz_scr[:, tq:] = jnp.where(row < DIFF_DH, 0.0, qt).astype(BF16)
    lane = lax.broadcasted_iota(jnp.int32, (1, LANES), 1)
    m_scr[...] = jnp.full_like(m_scr, -jnp.inf)
    acc_scr[...] = jnp.zeros_like(acc_scr)
    ones = jnp.ones((_ONES_ROWS, tq), BF16)
    key_row = lax.broadcasted_iota(jnp.int32, (tq, 1), 0)

    def scores(t, slot):
        ks = pl.multiple_of(t * tq, tq)
        s_scr[slot] = _dot(k_ref[pl.ds(ks, tq), :], qz_scr[...])

    def softmax(slot, masked):
        for c in range(2 * tq // LANES):
            cols = slice(c * LANES, (c + 1) * LANES)
            s = s_scr[slot, :, cols]
            if masked:
                s = jnp.where(key_row <= lane + (c * LANES) % tq, s, -jnp.inf)
            m_prev = m_scr[:, cols]
            m_new = jnp.maximum(m_prev, jnp.max(s, axis=0, keepdims=True))
            a_scr[slot, :, cols] = jnp.exp2(m_prev - m_new)
            m_scr[:, cols] = m_new
            p_scr[slot, :, cols] = jnp.exp2(s - m_new[0:1]).astype(BF16)

    def values(t, slot):
        v1t = jnp.concatenate([vt_ref[t], ones], axis=0)
        acc_scr[...] = acc_scr[...] * a_scr[slot, 0:1, :] + _dot(v1t, p_scr[slot])

    def step(t, slot):
        scores(t + 1, 1 - slot)
        softmax(slot, False)
        values(jnp.maximum(t - 1, 0), 1 - slot)

    def idle(slot):
        p_scr[slot] = jnp.zeros(p_scr.shape[1:], BF16)
        a_scr[slot] = jnp.ones(a_scr.shape[1:], F32)

    off = qi % 2

    @pl.when(off == 0)
    def _():
        idle(1)
        scores(0, 0)

    @pl.when(off == 1)
    def _():
        idle(0)
        scores(0, 1)
        step(0, 1)

    def body(u, carry):
        step(2 * u + off, 0)
        step(2 * u + 1 + off, 1)
        return carry

    lax.fori_loop(0, qi // 2, body, 0)
    softmax(0, True)
    values(jnp.maximum(qi - 1, 0), 1)
    values(qi, 0)

    acc = acc_scr[...]
    o = acc[0:DIFF_DV, :] / acc[DIFF_DV:DIFF_DV + 1, :]
    lam = _diff_lambda(lam_ref, lam_init)
    o_ref[...] = _diff_finish(o[:, 0:tq].T, o[:, tq:].T, lam, nw_ref[...], lam_init).astype(o_ref.dtype)


def _diff_prompt(lam_p, nw, q, k, vt, tq, lam_init):
    bsz, t, _ = q.shape
    qspec = pl.BlockSpec((None, tq, LANES), lambda b, h, i: (b, i, h))
    kspec = pl.BlockSpec((None, t, LANES), lambda b, h, i: (b, 0, h))
    vspec = pl.BlockSpec((None, t // tq, None, DIFF_DV, tq), lambda b, h, i: (b, 0, h, 0, 0))
    small = lambda a: pl.BlockSpec(a.shape, lambda b, h, i: (0,) * a.ndim)
    return pl.pallas_call(
        functools.partial(_diff_prompt_kernel, tq=tq, lam_init=lam_init),
        grid=(bsz, DIFF_HEADS, t // tq),
        in_specs=[small(lam_p), small(nw), qspec, kspec, vspec],
        out_specs=qspec,
        out_shape=jax.ShapeDtypeStruct((bsz, t, DIFF_V), BF16),
        scratch_shapes=[pltpu.VMEM((LANES, 2 * tq), BF16),
                        pltpu.VMEM((2, tq, 2 * tq), F32),
                        pltpu.VMEM((2, tq, 2 * tq), BF16),
                        pltpu.VMEM((8, 2 * tq), F32),
                        pltpu.VMEM((2, 8, 2 * tq), F32),
                        pltpu.VMEM((DIFF_DV + _ONES_ROWS, 2 * tq), F32)],
        compiler_params=_cparams(("parallel", "parallel", "arbitrary")),
        name="diff_prompt",
    )(lam_p, nw, q, k, vt)


_QROWS = 8


def _diff_sample_kernel(pt_ref, lam_ref, nw_ref, q_ref, kn_ref, vn_ref, *rest, pages, t_new, lam_init):
    k_refs, v_refs = rest[:pages], rest[pages:2 * pages]
    o_ref, qz_scr, m_scr, l_scr, acc_scr = rest[2 * pages:]
    j = pl.program_id(1)
    grp = 2 * _QROWS

    def update(keys, vals, mask):
        s = jnp.concatenate([_dot_nt(qz_scr[h * grp:(h + 1) * grp, :], keys[h]) for h in range(DIFF_HEADS)], axis=0)
        if mask is not None:
            s = jnp.where(mask, s, -jnp.inf)
        m_prev = m_scr[:, 0:1]
        m_new = jnp.maximum(m_prev, jnp.max(s, axis=1, keepdims=True))
        alpha = jnp.exp2(m_prev - m_new)
        p = jnp.exp2(s - m_new)
        l_new = l_scr[:, 0:1] * alpha + jnp.sum(p, axis=1, keepdims=True)
        pb = p.astype(BF16)
        pv = jnp.concatenate([_dot(pb[h * grp:(h + 1) * grp], vals[h]) for h in range(DIFF_HEADS)], axis=0)
        l_scr[...] = jnp.broadcast_to(l_new, l_scr.shape)
        acc_scr[...] = acc_scr[...] * alpha + pv
        m_scr[...] = jnp.broadcast_to(m_new, m_scr.shape)

    @pl.when(j == 0)
    def _():
        q = q_ref[...]
        lane = lax.broadcasted_iota(jnp.int32, (1, LANES), 1)
        m_scr[...] = jnp.full_like(m_scr, -jnp.inf)
        l_scr[...] = jnp.zeros_like(l_scr)
        acc_scr[...] = jnp.zeros_like(acc_scr)
        for h in range(DIFF_HEADS):
            qh = q[:, h * LANES:(h + 1) * LANES]
            qz_scr[h * grp:(h + 1) * grp, :] = jnp.concatenate(
                [jnp.where(lane < DIFF_DH, qh, 0.0), jnp.where(lane < DIFF_DH, 0.0, qh)], axis=0).astype(BF16)
        rr = lax.broadcasted_iota(jnp.int32, (DIFF_HEADS * grp, SAMPLE_PAD), 0) % _QROWS
        cc = lax.broadcasted_iota(jnp.int32, (DIFF_HEADS * grp, SAMPLE_PAD), 1)
        update([kn_ref[:, h * LANES:(h + 1) * LANES] for h in range(DIFF_HEADS)],
               [vn_ref[:, h * LANES:(h + 1) * LANES] for h in range(DIFF_HEADS)], (cc <= rr) & (cc < t_new))

    def head_of(refs, h):
        return jnp.concatenate([r[pl.ds(h, PAGE_SIZE, stride=DIFF_HEADS), :].astype(BF16) for r in refs], axis=0)

    update([head_of(k_refs, h) for h in range(DIFF_HEADS)], [head_of(v_refs, h) for h in range(DIFF_HEADS)], None)

    @pl.when(j == pl.num_programs(1) - 1)
    def _():
        acc = acc_scr[...]
        inv = 1.0 / l_scr[:, 0:1]
        lam = _diff_lambda(lam_ref, lam_init)
        nw = nw_ref[...]
        outs = []
        for h in range(DIFF_HEADS):
            r0 = h * grp
            r1 = r0 + _QROWS
            o0 = acc[r0:r0 + _QROWS] * inv[r0:r0 + _QROWS]
            o1 = acc[r1:r1 + _QROWS] * inv[r1:r1 + _QROWS]
            outs.append(_diff_finish(o0, o1, lam, nw, lam_init))
        o_ref[...] = jnp.concatenate(outs, axis=1)


def _diff_sample(page_table, lam_p, nw, q, k_new, v_new, cache_k, cache_v, layer, pages, t_new, lam_init):
    bsz, n_pages = page_table.shape
    small = lambda a: pl.BlockSpec(a.shape, lambda b, j, pt: (0,) * a.ndim)
    qspec = pl.BlockSpec((None, _QROWS, DIFF_QK), lambda b, j, pt: (b, 0, 0))
    nspec = pl.BlockSpec((None, SAMPLE_PAD, DIFF_QK), lambda b, j, pt: (b, 0, 0))

    def page_spec(i):
        return pl.BlockSpec((None, None, PAGE_SIZE * DIFF_HEADS, LANES),
                            lambda b, j, pt: (layer, pt[b, j * pages + i], 0, 0))

    nrow = 2 * DIFF_HEADS * _QROWS
    grid_spec = pltpu.PrefetchScalarGridSpec(
        num_scalar_prefetch=1,
        grid=(bsz, n_pages // pages),
        in_specs=[small(lam_p), small(nw), qspec, nspec, nspec]
                 + [page_spec(i) for i in range(pages)] + [page_spec(i) for i in range(pages)],
        out_specs=qspec,
        scratch_shapes=[pltpu.VMEM((nrow, LANES), BF16), pltpu.VMEM((nrow, LANES), F32),
                        pltpu.VMEM((nrow, LANES), F32), pltpu.VMEM((nrow, DIFF_DV), F32)],
    )
    return pl.pallas_call(
        functools.partial(_diff_sample_kernel, pages=pages, t_new=t_new, lam_init=lam_init),
        grid_spec=grid_spec,
        out_shape=jax.ShapeDtypeStruct((bsz, _QROWS, DIFF_V), F32),
        compiler_params=_cparams(("parallel", "arbitrary")),
        name="diff_sample",
    )(page_table, lam_p, nw, q, k_new, v_new, *([cache_k] * pages), *([cache_v] * pages))


_MEM_SPLIT = D_MODEL // LANES
_MEM_PIECES = MEM_DH // LANES


def _mem_row(head, piece):
    return piece * MEM_HEADS + head


def _mem_rows_view(a):
    lead = a.shape[:-3]
    n = len(lead)
    a = a.reshape(*lead, MEM_LEN, MEM_HEADS, _MEM_PIECES, LANES)
    a = a.transpose(*range(n), n, n + 2, n + 1, n + 3)
    return a.reshape(*lead, MEM_LEN * _MEM_SPLIT, LANES)


def _mem_rows_unview(a):
    lead = a.shape[:-2]
    n = len(lead)
    a = a.reshape(*lead, MEM_LEN, _MEM_PIECES, MEM_HEADS, LANES)
    a = a.transpose(*range(n), n, n + 2, n + 1, n + 3)
    return a.reshape(*lead, MEM_LEN, MEM_HEADS, MEM_DH)


def _memkv_kernel(x_ref, wk_ref, wv_ref, k_ref, v_ref):
    x = x_ref[...].astype(BF16)
    m = x.shape[0]
    for w_ref, o_ref in ((wk_ref, k_ref), (wv_ref, v_ref)):
        y = _dot(x, w_ref[...])
        for j in range(_MEM_SPLIT):
            o_ref[pl.ds(_mem_row(j // _MEM_PIECES, j % _MEM_PIECES), m, stride=_MEM_SPLIT), :] = (
                y[:, j * LANES:(j + 1) * LANES])


def _memkv(mem, wk, wv):
    m = mem.shape[0]
    full = lambda a: pl.BlockSpec(a.shape, lambda i: (0, 0))
    out = pl.BlockSpec((m * _MEM_SPLIT, LANES), lambda i: (0, 0))
    return pl.pallas_call(
        _memkv_kernel,
        grid=(1,),
        in_specs=[full(mem), full(wk), full(wv)],
        out_specs=[out, out],
        out_shape=[jax.ShapeDtypeStruct((m * _MEM_SPLIT, LANES), F32)] * 2,
        compiler_params=_cparams(("arbitrary",)),
        name="memkv",
    )(mem, wk, wv)


def _mix_proj(g_ref, d_ref, x_ref, wo1_ref, ln1g_ref, ln1b_ref, wq_ref, alpha):
    mix = _dot(g_ref[...], wo1_ref[0:GLA_V, :]) + _dot(d_ref[...], wo1_ref[GLA_V:, :])
    x1 = _post_ln(x_ref[...], mix, ln1g_ref[...], ln1b_ref[...], alpha)
    return x1, (_dot(x1.astype(BF16), wq_ref[...]) * ((MEM_DH ** -0.5) * LOG2E)).astype(BF16)


def _cross_heads(q, mk_ref, mv_ref):
    heads = []
    for h in range(MEM_HEADS):
        def head_of(ref):
            return jnp.concatenate([ref[pl.ds(_mem_row(h, j), MEM_LEN, stride=_MEM_SPLIT), :]
                                    for j in range(_MEM_PIECES)], axis=1).astype(BF16)

        s = _dot_nt(q[:, h * MEM_DH:(h + 1) * MEM_DH], head_of(mk_ref))
        p = jnp.exp2(s - jnp.max(s, axis=-1, keepdims=True))
        l = jnp.sum(p, axis=-1, keepdims=True)
        heads.append((_dot(p.astype(BF16), head_of(mv_ref)) / l).astype(BF16))
    return jnp.concatenate(heads, axis=1)


def _mix_cross_kernel(g_ref, d_ref, x_ref, wo1_ref, ln1g_ref, ln1b_ref, wq_ref, mk_ref, mv_ref, wo2_ref,
                      ln2g_ref, ln2b_ref, o_ref, *, alpha):
    x1, q = _mix_proj(g_ref, d_ref, x_ref, wo1_ref, ln1g_ref, ln1b_ref, wq_ref, alpha)
    y = _dot(_cross_heads(q, mk_ref, mv_ref), wo2_ref[...])
    o_ref[...] = _post_ln(x1, y, ln2g_ref[...], ln2b_ref[...], alpha)


def _mix_proj_kernel(g_ref, d_ref, x_ref, wo1_ref, ln1g_ref, ln1b_ref, wq_ref, x1_ref, q_ref, *, alpha):
    x1_ref[...], q_ref[...] = _mix_proj(g_ref, d_ref, x_ref, wo1_ref, ln1g_ref, ln1b_ref, wq_ref, alpha)


def _cross_kernel(q_ref, mk_ref, mv_ref, o_ref):
    o_ref[...] = _cross_heads(q_ref[...], mk_ref, mv_ref)


def _out_proj_kernel(o_ref, x1_ref, wo2_ref, ln2g_ref, ln2b_ref, y_ref, *, alpha):
    y_ref[...] = _post_ln(x1_ref[...], _dot(o_ref[...], wo2_ref[...]), ln2g_ref[...], ln2b_ref[...], alpha)


def _mix_cross_split(g, d, x, w_out, ln1g, ln1b, wq, mk, mv, mem_layer, wo, ln2g, ln2b, alpha):
    bsz, r, _ = x.shape
    rows = bsz * r
    flat = lambda a: a.reshape(rows, a.shape[-1])
    full = lambda a: pl.BlockSpec(a.shape, lambda *_: (0,) * a.ndim)
    x1, q = pl.pallas_call(
        functools.partial(_mix_proj_kernel, alpha=alpha),
        grid=(1,),
        in_specs=[full(flat(g)), full(flat(d)), full(flat(x)), full(w_out), full(ln1g), full(ln1b), full(wq)],
        out_specs=[full(flat(x)), full(flat(x))],
        out_shape=[jax.ShapeDtypeStruct((rows, D_MODEL), F32), jax.ShapeDtypeStruct((rows, D_MODEL), BF16)],
        compiler_params=_cparams(("arbitrary",)),
        name="mix_proj",
    )(flat(g), flat(d), flat(x), w_out, ln1g, ln1b, wq)
    seq = pl.BlockSpec((None, r, D_MODEL), lambda b: (b, 0, 0))
    mem = pl.BlockSpec((None, None, MEM_LEN * _MEM_SPLIT, LANES), lambda b: (mem_layer, b, 0, 0))
    o = pl.pallas_call(
        _cross_kernel,
        grid=(bsz,),
        in_specs=[seq, mem, mem],
        out_specs=seq,
        out_shape=jax.ShapeDtypeStruct((bsz, r, D_MODEL), BF16),
        compiler_params=_cparams(("parallel",)),
        name="cross",
    )(q.reshape(bsz, r, D_MODEL), mk, mv)
    y = pl.pallas_call(
        functools.partial(_out_proj_kernel, alpha=alpha),
        grid=(1,),
        in_specs=[full(flat(x)), full(flat(x)), full(wo), full(ln2g), full(ln2b)],
        out_specs=full(flat(x)),
        out_shape=jax.ShapeDtypeStruct((rows, D_MODEL), F32),
        compiler_params=_cparams(("arbitrary",)),
        name="out_proj",
    )(flat(o), x1, wo, ln2g, ln2b)
    return y.reshape(bsz, r, D_MODEL)


def _mix_cross(g, d, x, w_out, ln1g, ln1b, wq, mk, mv, mem_layer, wo, ln2g, ln2b, tm, alpha):
    bsz, r, _ = x.shape
    row = lambda n: pl.BlockSpec((None, tm, n), lambda b, i: (b, i, 0))
    full = lambda a: pl.BlockSpec(a.shape, lambda b, i: (0,) * a.ndim)
    mem = pl.BlockSpec((None, None, MEM_LEN * _MEM_SPLIT, LANES), lambda b, i: (mem_layer, b, 0, 0))
    return pl.pallas_call(
        functools.partial(_mix_cross_kernel, alpha=alpha),
        grid=(bsz, r // tm),
        in_specs=[row(GLA_V), row(DIFF_V), row(D_MODEL), full(w_out), full(ln1g), full(ln1b), full(wq),
                  mem, mem, full(wo), full(ln2g), full(ln2b)],
        out_specs=row(D_MODEL),
        out_shape=jax.ShapeDtypeStruct(x.shape, F32),
        compiler_params=_cparams(("parallel", "parallel")),
        name="mix_cross",
    )(g, d, x, w_out, ln1g, ln1b, wq, mk, mv, wo, ln2g, ln2b)


_CARRY = 8
_FF_CHUNK = 256


def _ffn_kernel(x_ref, st_ref, wua_ref, wub_ref, cw_ref, cb_ref, wd_ref, g_ref, b_ref,
                o_ref, ns_ref, carry_scr, h_scr, *, tm, stride, n_valid, alpha):
    i = pl.program_id(1)
    x = x_ref[...]
    xb = x.astype(BF16)
    rows8 = lax.broadcasted_iota(jnp.int32, (_CARRY, 1), 0)

    if stride == 1:
        @pl.when(i == 0)
        def _():
            for half in range(2):
                carry_scr[half, _CARRY - 2:_CARRY - 1, :] = st_ref[half, 0]
                carry_scr[half, _CARRY - 1:_CARRY, :] = st_ref[half, 1]

    for c in range(D_FF // _FF_CHUNK):
        cs = slice(c * _FF_CHUNK, (c + 1) * _FF_CHUNK)
        halves = []
        for half, w_ref in enumerate((wua_ref, wub_ref)):
            u = _dot(xb, w_ref[:, cs])
            if stride == 1:
                p1 = carry_scr[half, _CARRY - 1:_CARRY, cs]
                p2 = carry_scr[half, _CARRY - 2:_CARRY - 1, cs]
                r1, r2 = pltpu.roll(u, 1, 0), pltpu.roll(u, 2, 0)
                u1 = jnp.concatenate([jnp.where(rows8 == 0, p1, r1[0:_CARRY]), r1[_CARRY:]], axis=0)
                u2 = jnp.concatenate(
                    [jnp.where(rows8 == 0, p2, jnp.where(rows8 == 1, p1, r2[0:_CARRY])), r2[_CARRY:]], axis=0)
                carry_scr[half, :, cs] = u[tm - _CARRY:tm, :]
            else:
                padded = jnp.concatenate([st_ref[half, 0, :, cs], st_ref[half, 1, :, cs], u], axis=0)
                u1 = padded[stride:stride + tm]
                u2 = padded[0:tm]
            cw = cw_ref[half, :, cs]
            halves.append(cb_ref[half, :, cs] + cw[0:1] * u2 + cw[1:2] * u1 + cw[2:3] * u)
            for j in range(CONV_W - 1):
                r0 = (n_valid - (CONV_W - 1) + j) * stride
                ns_ref[half, j, :, cs] = u[r0:r0 + stride, :]
        a, bb = halves
        h_scr[:, cs] = (a * (1.0 / (1.0 + jnp.exp(-a))) * bb).astype(BF16)
    o_ref[...] = _post_ln(x, _dot(h_scr[...], wd_ref[...]), g_ref[...], b_ref[...], alpha)


def _ffn(x, state, w_up, conv_w, conv_b, w_down, g, b, tm, stride, n_valid, alpha):
    bsz, r, _ = x.shape
    assert stride == 1 or r == tm
    row = pl.BlockSpec((None, tm, D_MODEL), lambda bi, i: (bi, i, 0))
    st = pl.BlockSpec((None, 2, CONV_W - 1, stride, D_FF), lambda bi, i: (bi, 0, 0, 0, 0))
    once = pl.Buffered(1)
    full = lambda a: pl.BlockSpec(a.shape, lambda bi, i: (0,) * a.ndim, pipeline_mode=once)
    return pl.pallas_call(
        functools.partial(_ffn_kernel, tm=tm, stride=stride, n_valid=n_valid, alpha=alpha),
        grid=(bsz, r // tm),
        in_specs=[row, st,
                  pl.BlockSpec((D_MODEL, D_FF), lambda bi, i: (0, 0), pipeline_mode=once),
                  pl.BlockSpec((D_MODEL, D_FF), lambda bi, i: (0, 1), pipeline_mode=once),
                  full(conv_w), full(conv_b), full(w_down), full(g), full(b)],
        out_specs=[row, pl.BlockSpec((None, None, 2, CONV_W - 1, stride, D_FF), lambda bi, i: (bi, i, 0, 0, 0, 0))],
        out_shape=[jax.ShapeDtypeStruct(x.shape, F32),
                   jax.ShapeDtypeStruct((bsz, r // tm, 2, CONV_W - 1, stride, D_FF), F32)],
        scratch_shapes=[pltpu.VMEM((2, _CARRY, D_FF), F32), pltpu.VMEM((tm, D_FF), BF16)],
        compiler_params=_cparams(("parallel", "arbitrary")),
        name="ffn",
    )(x, state, w_up, w_up, conv_w, conv_b, w_down, g, b)


def _rope_tables(pos):
    half = DIFF_DH // 2
    inv = ROPE_THETA ** (-jnp.arange(half, dtype=F32) / half)
    ang = pos.astype(F32)[:, None] * inv[None, :]
    cos = jnp.tile(jnp.cos(ang), (1, LANES // half))
    sin = jnp.tile(jnp.sin(ang), (1, LANES // half))
    first = (jnp.arange(LANES) % DIFF_DH) < half
    return cos, jnp.where(first, -sin, 0.0), jnp.where(first, 0.0, sin)


def _conv_state_in(s):
    return s.reshape(s.shape[0], CONV_W - 1, 2, D_FF).transpose(2, 1, 0, 3)[None]


def _conv_state_out(s):
    return s[0].transpose(2, 1, 0, 3).reshape(s.shape[3], CONV_W - 1, 2 * D_FF)


def kernel(x_prompt, x_sample, mem_prompt, cache_k, cache_v, page_table, cache_mem_k, cache_mem_v,
           state_gla, state_conv, w_in, gla_wa2, gla_ba, gla_norm_w, diff_lambda, diff_norm_w, w_out,
           ln1_g, ln1_b, cross_wq, cross_wk, cross_wv, cross_wo, ln2_g, ln2_b,
           ffn_w_up, ffn_conv_w, ffn_conv_b, ffn_w_down, ln3_g, ln3_b):
    depth = w_in.shape[0]
    bp, tp, _ = x_prompt.shape
    bs, ts, _ = x_sample.shape
    n_pages = page_table.shape[1]
    past_len = n_pages * PAGE_SIZE
    alpha = (2 * depth) ** 0.25
    assert ts <= _QROWS and tp % 512 == 0

    tm_p = 1024
    tq = 512
    gla_tt = 512
    pages = math.gcd(n_pages, 16)

    tabs_p = _rope_tables(jnp.arange(tp))
    pos_s = past_len + jnp.minimum(jnp.arange(SAMPLE_PAD), ts - 1)
    tabs_s = _rope_tables(jnp.tile(pos_s, bs))

    xp = x_prompt.reshape(bp * tp, D_MODEL)
    xs = jnp.pad(x_sample, ((0, 0), (0, SAMPLE_PAD - ts), (0, 0))).reshape(bs * SAMPLE_PAD, D_MODEL)
    mem2 = mem_prompt.reshape(bp * MEM_LEN, D_MODEL)
    zero_conv = jnp.zeros((bp, 2, CONV_W - 1, 1, D_FF), F32)
    n_phys = cache_k.shape[1]
    cache_k4 = cache_k.reshape(depth, n_phys, PAGE_SIZE * DIFF_HEADS, 2 * DIFF_DH)
    cache_v4 = cache_v.reshape(depth, n_phys, PAGE_SIZE * DIFF_HEADS, DIFF_DV)
    cmem_k = _mem_rows_view(cache_mem_k)
    cmem_v = _mem_rows_view(cache_mem_v)

    outs = [[] for _ in range(10)]
    for l in range(depth):
        lam_init = 0.8 - 0.6 * math.exp(-0.3 * l)
        wl = w_in[l]
        ga0 = 2 * GLA_QK + 2 * GLA_V
        w_in_l = jnp.concatenate(
            [wl[:, :ga0], wl[:, ga0 + GLA_RANK:], wl[:, ga0:ga0 + GLA_RANK],
             jnp.zeros((D_MODEL, LANES - GLA_RANK), F32)], axis=1).astype(BF16)
        wa2_l = jnp.pad(gla_wa2[l], ((0, LANES - GLA_RANK), (0, 0))).astype(BF16)
        ba_l = gla_ba[l].reshape(1, GLA_QK)
        gnw = gla_norm_w[l].reshape(1, GLA_DV)
        dnw = diff_norm_w[l].reshape(1, DIFF_DV)
        w_out_l = w_out[l].astype(BF16)
        wq_l = cross_wq[l].astype(BF16)
        wk_l = cross_wk[l].astype(BF16)
        wv_l = cross_wv[l].astype(BF16)
        wo_l = cross_wo[l].astype(BF16)
        w_up_l = ffn_w_up[l].astype(BF16)
        w_down_l = ffn_w_down[l].astype(BF16)
        cw_l = ffn_conv_w[l].reshape(CONV_W, 2, D_FF).transpose(1, 0, 2)
        cb_l = ffn_conv_b[l].reshape(2, 1, D_FF)
        r1 = lambda a: a[l].reshape(1, D_MODEL)
        ln = [r1(a) for a in (ln1_g, ln1_b, ln2_g, ln2_b, ln3_g, ln3_b)]

        gq, gk, gg, gv, gr, dq, dkf, dvf, dkb, _, dvt = _inproj(xp, w_in_l, wa2_l, ba_l, tabs_p, tp // tq, tq)
        b3 = lambda a: a.reshape(bp, tp, a.shape[-1])
        go, s_p = _gla(b3(gq), b3(gk), b3(gg), b3(gv), b3(gr), gnw, None, bp, gla_tt, GLA_CHUNK, GLA_SUB, None)
        do = _diff_prompt(diff_lambda[l], dnw, b3(dq), b3(dkb),
                          dvt.reshape(bp, tp // tq, DIFF_HEADS, DIFF_DV, tq), tq, lam_init)
        mk_p, mv_p = _memkv(mem2, wk_l, wv_l)
        x2 = _mix_cross(go, do, xp.reshape(bp, tp, D_MODEL), w_out_l, ln[0], ln[1], wq_l,
                        mk_p.reshape(1, bp, MEM_LEN * _MEM_SPLIT, LANES),
                        mv_p.reshape(1, bp, MEM_LEN * _MEM_SPLIT, LANES), 0,
                        wo_l, ln[2], ln[3], tm_p, alpha)
        x3, conv_p = _ffn(x2, zero_conv, w_up_l, cw_l, cb_l, w_down_l, ln[4], ln[5], tm_p, 1, tm_p, alpha)
        xp = x3.reshape(bp * tp, D_MODEL)
        outs[0].append(dkf.reshape(bp, tp // PAGE_SIZE, PAGE_SIZE, DIFF_HEADS, 2 * DIFF_DH))
        outs[1].append(dvf.reshape(bp, tp // PAGE_SIZE, PAGE_SIZE, DIFF_HEADS, DIFF_DV))
        outs[4].append(s_p)
        outs[6].append(conv_p[:, -1].transpose(0, 2, 1, 3, 4).reshape(bp, CONV_W - 1, 2 * D_FF))
        outs[8].append(_mem_rows_unview(mk_p.reshape(bp, MEM_LEN * _MEM_SPLIT, LANES)))
        outs[9].append(_mem_rows_unview(mv_p.reshape(bp, MEM_LEN * _MEM_SPLIT, LANES)))

        gq, gk, gg, gv, gr, dq, dkf, dvf, dkb, dvb, _ = _inproj(xs, w_in_l, wa2_l, ba_l, tabs_s, 1, bs * SAMPLE_PAD)
        s3 = lambda a: a.reshape(bs, SAMPLE_PAD, a.shape[-1])
        go, s_s = _gla(s3(gq), s3(gk), s3(gg), s3(gv), s3(gr), gnw, state_gla[l],
                       math.gcd(bs, 8), SAMPLE_PAD, SAMPLE_PAD, SAMPLE_PAD, ts)
        dq_s = s3(dq)[:, :_QROWS].astype(F32)
        do8 = _diff_sample(page_table, diff_lambda[l], dnw, dq_s, s3(dkb), s3(dvb),
                           cache_k4, cache_v4, l, pages, ts, lam_init)
        do = jnp.pad(do8, ((0, 0), (0, SAMPLE_PAD - _QROWS), (0, 0))).astype(BF16)
        x2 = _mix_cross_split(go, do, xs.reshape(bs, SAMPLE_PAD, D_MODEL), w_out_l, ln[0], ln[1], wq_l,
                              cmem_k, cmem_v, l, wo_l, ln[2], ln[3], alpha)
        x2t = x2.transpose(1, 0, 2).reshape(1, SAMPLE_PAD * bs, D_MODEL)
        x3t, conv_s = _ffn(x2t, _conv_state_in(state_conv[l]), w_up_l, cw_l, cb_l, w_down_l, ln[4], ln[5],
                           SAMPLE_PAD * bs, bs, ts, alpha)
        xs = x3t.reshape(SAMPLE_PAD, bs, D_MODEL).transpose(1, 0, 2).reshape(bs * SAMPLE_PAD, D_MODEL)
        outs[2].append(dkf.reshape(bs, SAMPLE_PAD, DIFF_HEADS, 2 * DIFF_DH)[:, :ts])
        outs[3].append(dvf.reshape(bs, SAMPLE_PAD, DIFF_HEADS, DIFF_DV)[:, :ts])
        outs[5].append(s_s)
        outs[7].append(_conv_state_out(conv_s[:, -1]))

    st = [jnp.stack(o) for o in outs]
    y_prompt = xp.reshape(bp, tp, D_MODEL)
    y_sample = xs.reshape(bs, SAMPLE_PAD, D_MODEL)[:, :ts]
    return (y_prompt, y_sample, st[0], st[1], st[2], st[3], st[4], st[5], st[6], st[7], st[8], st[9])
```

```python
import functools
import math

import jax
import jax.numpy as jnp
from jax import lax
from jax.experimental import pallas as pl
from jax.experimental.pallas import tpu as pltpu

F32 = jnp.float32
BF16 = jnp.bfloat16

D_MODEL = 1024
PAGE_SIZE = 128
GLA_HEADS = 4
GLA_DK = 64
GLA_DV = 128
GLA_RANK = 16
GLA_TAU = 16.0
GLA_CHUNK = 64
GLA_SUB = 16
DIFF_HEADS = 4
DIFF_DH = 64
DIFF_DV = 128
ROPE_THETA = 10000.0
MEM_LEN = 256
MEM_HEADS = 4
MEM_DH = D_MODEL // MEM_HEADS
D_FF = 2816
CONV_W = 3
LN_EPS = 1e-5
LOG2E = 1.4426950408889634

GLA_QK = GLA_HEADS * GLA_DK
GLA_V = GLA_HEADS * GLA_DV
DIFF_QK = DIFF_HEADS * 2 * DIFF_DH
DIFF_V = DIFF_HEADS * DIFF_DV
LANES = 128
N_IN_PAD = 2 * GLA_QK + 2 * GLA_V + 2 * DIFF_QK + DIFF_V + LANES
SAMPLE_PAD = 16
VMEM_LIMIT = 52 * 1024 * 1024


def _cparams(sem):
    return pltpu.CompilerParams(dimension_semantics=sem, vmem_limit_bytes=VMEM_LIMIT)


def _post_ln(res, sub, g, b, alpha):
    h = alpha * res + sub
    mu = jnp.mean(h, axis=-1, keepdims=True)
    hc = h - mu
    var = jnp.mean(hc * hc, axis=-1, keepdims=True)
    return hc * lax.rsqrt(var + LN_EPS) * g + b


def _dot(a, b):
    return jnp.dot(a, b, preferred_element_type=F32)


def _dot_nt(a, b):
    return lax.dot_general(a, b, (((1,), (1,)), ((), ())), preferred_element_type=F32)


def _dot_tn(a, b):
    return lax.dot_general(a, b, (((0,), (0,)), ((), ())), preferred_element_type=F32)


_O_GQ, _O_GK, _O_GV, _O_GR = 0, GLA_QK, 2 * GLA_QK, 2 * GLA_QK + GLA_V
_O_DQ = 2 * GLA_QK + 2 * GLA_V
_O_DK = _O_DQ + DIFF_QK
_O_DV = _O_DK + DIFF_QK
_O_GA = _O_DV + DIFF_V


def _inproj_kernel(x_ref, w_ref, wa2_ref, ba_ref, c_ref, sa_ref, sb_ref,
                   gq_ref, gk_ref, gg_ref, gv_ref, gr_ref, dq_ref, dkf_ref, dvf_ref, dkb_ref, dvb_ref, dvt_ref):
    h = _dot(x_ref[...].astype(BF16), w_ref[...])
    gq_ref[...] = h[:, _O_GQ:_O_GQ + GLA_QK] * (GLA_DK ** -0.5)
    gk_ref[...] = h[:, _O_GK:_O_GK + GLA_QK]
    gv_ref[...] = h[:, _O_GV:_O_GV + GLA_V].astype(BF16)
    gr_ref[...] = h[:, _O_GR:_O_GR + GLA_V]
    xg = _dot(h[:, _O_GA:_O_GA + LANES].astype(BF16), wa2_ref[...]) + ba_ref[...]
    gg_ref[...] = (jnp.minimum(xg, 0.0) - jnp.log1p(jnp.exp(-jnp.abs(xg)))) * (1.0 / GLA_TAU)
    c, sa, sb = c_ref[...], sa_ref[...], sb_ref[...]

    def rope(v):
        return v * c + pltpu.roll(v, LANES - DIFF_DH // 2, 1) * sa + pltpu.roll(v, DIFF_DH // 2, 1) * sb

    qscale = (DIFF_DH ** -0.5) * LOG2E
    tm = h.shape[0]
    for j in range(DIFF_HEADS):
        sl = slice(j * LANES, (j + 1) * LANES)
        head_rows = pl.ds(j, tm, stride=DIFF_HEADS)
        dq_ref[:, sl] = (rope(h[:, _O_DQ + j * LANES:_O_DQ + (j + 1) * LANES]) * qscale).astype(BF16)
        rk = rope(h[:, _O_DK + j * LANES:_O_DK + (j + 1) * LANES])
        dkf_ref[head_rows, :] = rk
        dkb_ref[:, sl] = rk.astype(BF16)
        dv = h[:, _O_DV + j * LANES:_O_DV + (j + 1) * LANES]
        dvf_ref[head_rows, :] = dv
        dvb_ref[:, sl] = dv.astype(BF16)
        dvt_ref[j] = dv.T.astype(BF16)


def _inproj(x, w, wa2, ba, tabs, tab_blocks, tm):
    m = x.shape[0]
    row = lambda n: pl.BlockSpec((tm, n), lambda i: (i, 0))
    full = lambda a: pl.BlockSpec(a.shape, lambda i: (0,) * a.ndim)
    tab = pl.BlockSpec((tm, LANES), lambda i: (i % tab_blocks, 0))
    shapes = [(1, GLA_QK, F32), (1, GLA_QK, F32), (1, GLA_QK, F32), (1, GLA_V, BF16), (1, GLA_V, F32),
              (1, DIFF_QK, BF16), (DIFF_HEADS, LANES, F32), (DIFF_HEADS, LANES, F32),
              (1, DIFF_QK, BF16), (1, DIFF_V, BF16)]
    out_specs = [pl.BlockSpec((tm * k, n), lambda i: (i, 0)) for k, n, _ in shapes]
    out_shape = [jax.ShapeDtypeStruct((m * k, n), dt) for k, n, dt in shapes]
    out_specs.append(pl.BlockSpec((None, DIFF_HEADS, DIFF_DV, tm), lambda i: (i, 0, 0, 0)))
    out_shape.append(jax.ShapeDtypeStruct((m // tm, DIFF_HEADS, DIFF_DV, tm), BF16))
    return pl.pallas_call(
        _inproj_kernel,
        grid=(m // tm,),
        in_specs=[row(D_MODEL), full(w), full(wa2), full(ba), tab, tab, tab],
        out_specs=out_specs,
        out_shape=out_shape,
        compiler_params=_cparams(("parallel",)),
        name="inproj",
    )(x, w, wa2, ba, *tabs)


def _bdot(a, b, contract):
    return lax.dot_general(a, b, (((contract[0],), (contract[1],)), ((0,), (0,))), preferred_element_type=F32)


def _gla_chunk(qc, kc, gc, vc, st, tril, head_a, st_mask, chunk, sub):
    n = qc.shape[0]
    nsub = chunk // sub
    g1 = gc.astype(BF16)
    r1 = gc - g1.astype(F32)
    g2 = r1.astype(BF16)
    g3 = (r1 - g2.astype(F32)).astype(BF16)
    b3 = _dot(tril, jnp.concatenate([g[i] for i in range(n) for g in (g1, g2, g3)], axis=1))
    b = jnp.stack([b3[:, (3 * i) * LANES:(3 * i + 1) * LANES] + b3[:, (3 * i + 1) * LANES:(3 * i + 2) * LANES]
                   + b3[:, (3 * i + 2) * LANES:(3 * i + 3) * LANES] for i in range(n)])
    b_last = b[:, chunk - 1:chunk, :]

    def by_head(x):
        return jnp.concatenate([jnp.where(head_a, x, 0.0), jnp.where(head_a, 0.0, x)], axis=1).astype(BF16)

    oi = _bdot(by_head(qc * jnp.exp(b)), st.astype(BF16), (2, 2))
    o_inter = jnp.concatenate([oi[:, 0:chunk, 0:GLA_DV], oi[:, chunk:, GLA_DV:]], axis=2)
    key_row = lax.broadcasted_iota(jnp.int32, (1, chunk, 1), 1)
    lqs, kss = [], []
    for s in range(nsub):
        r0, r1_ = s * sub, (s + 1) * sub
        anchor = b[:, r0:r0 + 1, :]
        lqs.append(by_head(qc[:, r0:r1_] * jnp.exp(b[:, r0:r1_] - anchor)))
        kss.append((kc * jnp.exp(jnp.where(key_row < r1_, anchor - b, 0.0))).astype(BF16))
    att = _bdot(jnp.concatenate(lqs, axis=0), jnp.concatenate(kss, axis=0), (2, 2))
    sub_idx = lax.broadcasted_iota(jnp.int32, att.shape, 0) // n
    q_tok = sub_idx * sub + lax.broadcasted_iota(jnp.int32, att.shape, 1) % sub
    att = jnp.where(lax.broadcasted_iota(jnp.int32, att.shape, 2) <= q_tok, att, 0.0).astype(BF16)
    att = jnp.concatenate([att[s * n:(s + 1) * n] for s in range(nsub)], axis=1)
    ov = _bdot(att, vc, (2, 1))
    o_intra = jnp.concatenate(
        [jnp.concatenate([ov[:, 2 * s * sub:(2 * s + 1) * sub, 0:GLA_DV],
                          ov[:, (2 * s + 1) * sub:(2 * s + 2) * sub, GLA_DV:]], axis=2) for s in range(nsub)], axis=1)
    upd = _bdot(vc, (kc * jnp.exp(b_last - b)).astype(BF16), (1, 1))
    return o_inter + o_intra, st * jnp.exp(b_last) + jnp.where(st_mask, upd, 0.0)


def _gla_kernel(*refs, bb, tt, chunk, sub, t_valid, has_s0):
    if has_s0:
        q_ref, k_ref, g_ref, v_ref, gr_ref, nw_ref, s0_ref, o_ref, s_out_ref, st_scr = refs
    else:
        q_ref, k_ref, g_ref, v_ref, gr_ref, nw_ref, o_ref, s_out_ref, st_scr = refs
    i = pl.program_id(1)
    pairs = GLA_HEADS // 2
    head_a = lax.broadcasted_iota(jnp.int32, (1, LANES), 1) < GLA_DK
    st_mask = (lax.broadcasted_iota(jnp.int32, (2 * GLA_DV, LANES), 0) // GLA_DV
               == lax.broadcasted_iota(jnp.int32, (2 * GLA_DV, LANES), 1) // GLA_DK)
    tril = (lax.broadcasted_iota(jnp.int32, (chunk, chunk), 0)
            >= lax.broadcasted_iota(jnp.int32, (chunk, chunk), 1)).astype(BF16)
    row_in_chunk = lax.broadcasted_iota(jnp.int32, (chunk, 1), 0)

    @pl.when(i == 0)
    def _():
        for bi in range(bb):
            for p in range(pairs):
                if has_s0:
                    s0t = s0_ref[bi, 2 * p:2 * p + 2].reshape(2 * GLA_DK, GLA_DV).T
                    st_scr[bi, p] = jnp.where(st_mask, jnp.concatenate([s0t, s0t], axis=0), 0.0)
                else:
                    st_scr[bi, p] = jnp.zeros((2 * GLA_DV, LANES), F32)

    nw = nw_ref[...]
    chains = [(bi, p) for bi in range(bb) for p in range(pairs)]
    ks = lambda p: slice(p * LANES, (p + 1) * LANES)
    vs = lambda p: slice(p * 2 * GLA_DV, (p + 1) * 2 * GLA_DV)
    st = st_scr[...].reshape(len(chains), 2 * GLA_DV, LANES)
    for c in range(tt // chunk):
        rs = slice(c * chunk, (c + 1) * chunk)
        qc = jnp.stack([q_ref[bi, rs, ks(p)] for bi, p in chains])
        kc = jnp.stack([k_ref[bi, rs, ks(p)] for bi, p in chains])
        gc = jnp.stack([g_ref[bi, rs, ks(p)] for bi, p in chains])
        vc = jnp.stack([v_ref[bi, rs, vs(p)] for bi, p in chains])
        if t_valid is not None:
            ok = (row_in_chunk + c * chunk) < t_valid
            kc = jnp.where(ok, kc, 0.0)
            gc = jnp.where(ok, gc, 0.0)
            vc = jnp.where(ok, vc, jnp.zeros_like(vc))
        o, st = _gla_chunk(qc, kc, gc, vc, st, tril, head_a, st_mask, chunk, sub)
        gr = jnp.stack([gr_ref[bi, rs, vs(p)] for bi, p in chains])
        normed = []
        for hh in range(2):
            oh = o[:, :, hh * GLA_DV:(hh + 1) * GLA_DV]
            normed.append(oh * lax.rsqrt(jnp.mean(oh * oh, axis=-1, keepdims=True) + LN_EPS) * nw)
        out = (jnp.concatenate(normed, axis=2) * (gr * (1.0 / (1.0 + jnp.exp(-gr))))).astype(o_ref.dtype)
        for idx, (bi, p) in enumerate(chains):
            o_ref[bi, rs, vs(p)] = out[idx]
    st_scr[...] = st.reshape(st_scr.shape)

    @pl.when(i == pl.num_programs(1) - 1)
    def _():
        for bi in range(bb):
            for p in range(pairs):
                st = st_scr[bi, p]
                sa = st[0:GLA_DV].T
                sb = st[GLA_DV:].T
                s_out_ref[bi, 2 * p:2 * p + 2] = jnp.concatenate(
                    [sa[0:GLA_DK], sb[GLA_DK:]], axis=0).reshape(2, GLA_DK, GLA_DV)


def _gla(q, k, g, v, gr, nw, s0, bb, tt, chunk, sub, t_valid):
    bsz, t, _ = q.shape
    has_s0 = s0 is not None
    qk_spec = pl.BlockSpec((bb, tt, GLA_QK), lambda b, i: (b, i, 0))
    v_spec = pl.BlockSpec((bb, tt, GLA_V), lambda b, i: (b, i, 0))
    s_spec = pl.BlockSpec((bb, GLA_HEADS, GLA_DK, GLA_DV), lambda b, i: (b, 0, 0, 0))
    in_specs = [qk_spec, qk_spec, qk_spec, v_spec, v_spec, pl.BlockSpec(nw.shape, lambda b, i: (0, 0))]
    args = [q, k, g, v, gr, nw]
    if has_s0:
        in_specs.append(s_spec)
        args.append(s0)
    return pl.pallas_call(
        functools.partial(_gla_kernel, bb=bb, tt=tt, chunk=chunk, sub=sub, t_valid=t_valid, has_s0=has_s0),
        grid=(bsz // bb, t // tt),
        in_specs=in_specs,
        out_specs=[v_spec, s_spec],
        out_shape=[jax.ShapeDtypeStruct((bsz, t, GLA_V), BF16),
                   jax.ShapeDtypeStruct((bsz, GLA_HEADS, GLA_DK, GLA_DV), F32)],
        scratch_shapes=[pltpu.VMEM((bb, GLA_HEADS // 2, 2 * GLA_DV, LANES), F32)],
        compiler_params=_cparams(("parallel", "arbitrary")),
        name="gla",
    )(*args)


def _diff_lambda(lam_ref, lam_init):
    lf = lam_ref[...]
    a = jnp.sum(lf[0:1] * lf[1:2], axis=-1, keepdims=True)
    b = jnp.sum(lf[2:3] * lf[3:4], axis=-1, keepdims=True)
    return jnp.exp(a) - jnp.exp(b) + lam_init


def _diff_finish(o0, o1, lam, nw, lam_init):
    od = o0 - lam * o1
    ms = jnp.mean(od * od, axis=-1, keepdims=True)
    return od * lax.rsqrt(ms + LN_EPS) * nw * (1.0 - lam_init)


_ONES_ROWS = 16


def _diff_prompt_kernel(lam_ref, nw_ref, q_ref, k_ref, vt_ref, o_ref,
                        qz_scr, s_scr, p_scr, m_scr, a_scr, acc_scr, *, tq, lam_init):
    qi = pl.program_id(2)
    qt = q_ref[...].astype(F32).T
    row = lax.broadcasted_iota(jnp.int32, (LANES, 1), 0)
    qz_scr[:, 0:tq] = jnp.where(row < DIFF_DH, qt, 0.0).astype(BF16)
    qz_scr[:, tq:] = jnp.where(row < DIFF_DH, 0.0, qt).astype(BF16)
    lane = lax.broadcasted_iota(jnp.int32, (1, LANES), 1)
    m_scr[...] = jnp.full_like(m_scr, -jnp.inf)
    acc_scr[...] = jnp.zeros_like(acc_scr)
    ones = jnp.ones((_ONES_ROWS, tq), BF16)
    key_row = lax.broadcasted_iota(jnp.int32, (tq, 1), 0)

    def scores(t, slot):
        ks = pl.multiple_of(t * tq, tq)
        s_scr[slot] = _dot(k_ref[pl.ds(ks, tq), :], qz_scr[...])

    def softmax(slot, masked):
        for c in range(2 * tq // LANES):
            cols = slice(c * LANES, (c + 1) * LANES)
            s = s_scr[slot, :, cols]
            if masked:
                s = jnp.where(key_row <= lane + (c * LANES) % tq, s, -jnp.inf)
            m_prev = m_scr[:, cols]
            m_new = jnp.maximum(m_prev, jnp.max(s, axis=0, keepdims=True))
            a_scr[slot, :, cols] = jnp.exp2(m_prev - m_new)
            m_scr[:, cols] = m_new
            p_scr[slot, :, cols] = jnp.exp2(s - m_new[0:1]).astype(BF16)

    def values(t, slot):
        v1t = jnp.concatenate([vt_ref[t], ones], axis=0)
        acc_scr[...] = acc_scr[...] * a_scr[slot, 0:1, :] + _dot(v1t, p_scr[slot])

    def step(t, slot):
        scores(t + 1, 1 - slot)
        softmax(slot, False)
        values(jnp.maximum(t - 1, 0), 1 - slot)

    def idle(slot):
        p_scr[slot] = jnp.zeros(p_scr.shape[1:], BF16)
        a_scr[slot] = jnp.ones(a_scr.shape[1:], F32)

    off = qi % 2

    @pl.when(off == 0)
    def _():
        idle(1)
        scores(0, 0)

    @pl.when(off == 1)
    def _():
        idle(0)
        scores(0, 1)
        step(0, 1)

    def body(u, carry):
        step(2 * u + off, 0)
        step(2 * u + 1 + off, 1)
        return carry

    lax.fori_loop(0, qi // 2, body, 0)
    softmax(0, True)
    values(jnp.maximum(qi - 1, 0), 1)
    values(qi, 0)

    acc = acc_scr[...]
    o = acc[0:DIFF_DV, :] / acc[DIFF_DV:DIFF_DV + 1, :]
    lam = _diff_lambda(lam_ref, lam_init)
    o_ref[...] = _diff_finish(o[:, 0:tq].T, o[:, tq:].T, lam, nw_ref[...], lam_init).astype(o_ref.dtype)


def _diff_prompt(lam_p, nw, q, k, vt, tq, lam_init):
    bsz, t, _ = q.shape
    qspec = pl.BlockSpec((None, tq, LANES), lambda b, h, i: (b, i, h))
    kspec = pl.BlockSpec((None, t, LANES), lambda b, h, i: (b, 0, h))
    vspec = pl.BlockSpec((None, t // tq, None, DIFF_DV, tq), lambda b, h, i: (b, 0, h, 0, 0))
    small = lambda a: pl.BlockSpec(a.shape, lambda b, h, i: (0,) * a.ndim)
    return pl.pallas_call(
        functools.partial(_diff_prompt_kernel, tq=tq, lam_init=lam_init),
        grid=(bsz, DIFF_HEADS, t // tq),
        in_specs=[small(lam_p), small(nw), qspec, kspec, vspec],
        out_specs=qspec,
        out_shape=jax.ShapeDtypeStruct((bsz, t, DIFF_V), BF16),
        scratch_shapes=[pltpu.VMEM((LANES, 2 * tq), BF16),
                        pltpu.VMEM((2, tq, 2 * tq), F32),
                        pltpu.VMEM((2, tq, 2 * tq), BF16),
                        pltpu.VMEM((8, 2 * tq), F32),
                        pltpu.VMEM((2, 8, 2 * tq), F32),
                        pltpu.VMEM((DIFF_DV + _ONES_ROWS, 2 * tq), F32)],
        compiler_params=_cparams(("parallel", "parallel", "arbitrary")),
        name="diff_prompt",
    )(lam_p, nw, q, k, vt)


_QROWS = 8


def _diff_sample_kernel(pt_ref, lam_ref, nw_ref, q_ref, kn_ref, vn_ref, *rest, pages, t_new, lam_init):
    k_refs, v_refs = rest[:pages], rest[pages:2 * pages]
    o_ref, qz_scr, m_scr, l_scr, acc_scr = rest[2 * pages:]
    j = pl.program_id(1)
    grp = 2 * _QROWS

    def update(keys, vals, mask):
        s = jnp.concatenate([_dot_nt(qz_scr[h * grp:(h + 1) * grp, :], keys[h]) for h in range(DIFF_HEADS)], axis=0)
        if mask is not None:
            s = jnp.where(mask, s, -jnp.inf)
        m_prev = m_scr[:, 0:1]
        m_new = jnp.maximum(m_prev, jnp.max(s, axis=1, keepdims=True))
        alpha = jnp.exp2(m_prev - m_new)
        p = jnp.exp2(s - m_new)
        l_new = l_scr[:, 0:1] * alpha + jnp.sum(p, axis=1, keepdims=True)
        pb = p.astype(BF16)
        pv = jnp.concatenate([_dot(pb[h * grp:(h + 1) * grp], vals[h]) for h in range(DIFF_HEADS)], axis=0)
        l_scr[...] = jnp.broadcast_to(l_new, l_scr.shape)
        acc_scr[...] = acc_scr[...] * alpha + pv
        m_scr[...] = jnp.broadcast_to(m_new, m_scr.shape)

    @pl.when(j == 0)
    def _():
        q = q_ref[...]
        lane = lax.broadcasted_iota(jnp.int32, (1, LANES), 1)
        m_scr[...] = jnp.full_like(m_scr, -jnp.inf)
        l_scr[...] = jnp.zeros_like(l_scr)
        acc_scr[...] = jnp.zeros_like(acc_scr)
        for h in range(DIFF_HEADS):
            qh = q[:, h * LANES:(h + 1) * LANES]
            qz_scr[h * grp:(h + 1) * grp, :] = jnp.concatenate(
                [jnp.where(lane < DIFF_DH, qh, 0.0), jnp.where(lane < DIFF_DH, 0.0, qh)], axis=0).astype(BF16)
        rr = lax.broadcasted_iota(jnp.int32, (DIFF_HEADS * grp, SAMPLE_PAD), 0) % _QROWS
        cc = lax.broadcasted_iota(jnp.int32, (DIFF_HEADS * grp, SAMPLE_PAD), 1)
        update([kn_ref[:, h * LANES:(h + 1) * LANES] for h in range(DIFF_HEADS)],
               [vn_ref[:, h * LANES:(h + 1) * LANES] for h in range(DIFF_HEADS)], (cc <= rr) & (cc < t_new))

    def head_of(refs, h):
        return jnp.concatenate([r[pl.ds(h, PAGE_SIZE, stride=DIFF_HEADS), :].astype(BF16) for r in refs], axis=0)

    update([head_of(k_refs, h) for h in range(DIFF_HEADS)], [head_of(v_refs, h) for h in range(DIFF_HEADS)], None)

    @pl.when(j == pl.num_programs(1) - 1)
    def _():
        acc = acc_scr[...]
        inv = 1.0 / l_scr[:, 0:1]
        lam = _diff_lambda(lam_ref, lam_init)
        nw = nw_ref[...]
        outs = []
        for h in range(DIFF_HEADS):
            r0 = h * grp
            r1 = r0 + _QROWS
            o0 = acc[r0:r0 + _QROWS] * inv[r0:r0 + _QROWS]
            o1 = acc[r1:r1 + _QROWS] * inv[r1:r1 + _QROWS]
            outs.append(_diff_finish(o0, o1, lam, nw, lam_init))
        o_ref[...] = jnp.concatenate(outs, axis=1)


def _diff_sample(page_table, lam_p, nw, q, k_new, v_new, cache_k, cache_v, layer, pages, t_new, lam_init):
    bsz, n_pages = page_table.shape
    small = lambda a: pl.BlockSpec(a.shape, lambda b, j, pt: (0,) * a.ndim)
    qspec = pl.BlockSpec((None, _QROWS, DIFF_QK), lambda b, j, pt: (b, 0, 0))
    nspec = pl.BlockSpec((None, SAMPLE_PAD, DIFF_QK), lambda b, j, pt: (b, 0, 0))

    def page_spec(i):
        return pl.BlockSpec((None, None, PAGE_SIZE * DIFF_HEADS, LANES),
                            lambda b, j, pt: (layer, pt[b, j * pages + i], 0, 0))

    nrow = 2 * DIFF_HEADS * _QROWS
    grid_spec = pltpu.PrefetchScalarGridSpec(
        num_scalar_prefetch=1,
        grid=(bsz, n_pages // pages),
        in_specs=[small(lam_p), small(nw), qspec, nspec, nspec]
                 + [page_spec(i) for i in range(pages)] + [page_spec(i) for i in range(pages)],
        out_specs=qspec,
        scratch_shapes=[pltpu.VMEM((nrow, LANES), BF16), pltpu.VMEM((nrow, LANES), F32),
                        pltpu.VMEM((nrow, LANES), F32), pltpu.VMEM((nrow, DIFF_DV), F32)],
    )
    return pl.pallas_call(
        functools.partial(_diff_sample_kernel, pages=pages, t_new=t_new, lam_init=lam_init),
        grid_spec=grid_spec,
        out_shape=jax.ShapeDtypeStruct((bsz, _QROWS, DIFF_V), F32),
        compiler_params=_cparams(("parallel", "arbitrary")),
        name="diff_sample",
    )(page_table, lam_p, nw, q, k_new, v_new, *([cache_k] * pages), *([cache_v] * pages))


_MEM_SPLIT = D_MODEL // LANES
_MEM_PIECES = MEM_DH // LANES


def _mem_row(head, piece):
    return piece * MEM_HEADS + head


def _mem_rows_view(a):
    lead = a.shape[:-3]
    n = len(lead)
    a = a.reshape(*lead, MEM_LEN, MEM_HEADS, _MEM_PIECES, LANES)
    a = a.transpose(*range(n), n, n + 2, n + 1, n + 3)
    return a.reshape(*lead, MEM_LEN * _MEM_SPLIT, LANES)


def _mem_rows_unview(a):
    lead = a.shape[:-2]
    n = len(lead)
    a = a.reshape(*lead, MEM_LEN, _MEM_PIECES, MEM_HEADS, LANES)
    a = a.transpose(*range(n), n, n + 2, n + 1, n + 3)
    return a.reshape(*lead, MEM_LEN, MEM_HEADS, MEM_DH)


def _memkv_kernel(x_ref, wk_ref, wv_ref, k_ref, v_ref):
    x = x_ref[...].astype(BF16)
    m = x.shape[0]
    for w_ref, o_ref in ((wk_ref, k_ref), (wv_ref, v_ref)):
        y = _dot(x, w_ref[...])
        for j in range(_MEM_SPLIT):
            o_ref[pl.ds(_mem_row(j // _MEM_PIECES, j % _MEM_PIECES), m, stride=_MEM_SPLIT), :] = (
                y[:, j * LANES:(j + 1) * LANES])


def _memkv(mem, wk, wv):
    m = mem.shape[0]
    full = lambda a: pl.BlockSpec(a.shape, lambda i: (0, 0))
    out = pl.BlockSpec((m * _MEM_SPLIT, LANES), lambda i: (0, 0))
    return pl.pallas_call(
        _memkv_kernel,
        grid=(1,),
        in_specs=[full(mem), full(wk), full(wv)],
        out_specs=[out, out],
        out_shape=[jax.ShapeDtypeStruct((m * _MEM_SPLIT, LANES), F32)] * 2,
        compiler_params=_cparams(("arbitrary",)),
        name="memkv",
    )(mem, wk, wv)


def _mix_proj(g_ref, d_ref, x_ref, wo1_ref, ln1g_ref, ln1b_ref, wq_ref, alpha):
    mix = _dot(g_ref[...], wo1_ref[0:GLA_V, :]) + _dot(d_ref[...], wo1_ref[GLA_V:, :])
    x1 = _post_ln(x_ref[...], mix, ln1g_ref[...], ln1b_ref[...], alpha)
    return x1, (_dot(x1.astype(BF16), wq_ref[...]) * ((MEM_DH ** -0.5) * LOG2E)).astype(BF16)


def _cross_heads(q, mk_ref, mv_ref):
    heads = []
    for h in range(MEM_HEADS):
        def head_of(ref):
            return jnp.concatenate([ref[pl.ds(_mem_row(h, j), MEM_LEN, stride=_MEM_SPLIT), :]
                                    for j in range(_MEM_PIECES)], axis=1).astype(BF16)

        s = _dot_nt(q[:, h * MEM_DH:(h + 1) * MEM_DH], head_of(mk_ref))
        p = jnp.exp2(s - jnp.max(s, axis=-1, keepdims=True))
        l = jnp.sum(p, axis=-1, keepdims=True)
        heads.append((_dot(p.astype(BF16), head_of(mv_ref)) / l).astype(BF16))
    return jnp.concatenate(heads, axis=1)


def _mix_cross_kernel(g_ref, d_ref, x_ref, wo1_ref, ln1g_ref, ln1b_ref, wq_ref, mk_ref, mv_ref, wo2_ref,
                      ln2g_ref, ln2b_ref, o_ref, *, alpha):
    x1, q = _mix_proj(g_ref, d_ref, x_ref, wo1_ref, ln1g_ref, ln1b_ref, wq_ref, alpha)
    y = _dot(_cross_heads(q, mk_ref, mv_ref), wo2_ref[...])
    o_ref[...] = _post_ln(x1, y, ln2g_ref[...], ln2b_ref[...], alpha)


def _mix_proj_kernel(g_ref, d_ref, x_ref, wo1_ref, ln1g_ref, ln1b_ref, wq_ref, x1_ref, q_ref, *, alpha):
    x1_ref[...], q_ref[...] = _mix_proj(g_ref, d_ref, x_ref, wo1_ref, ln1g_ref, ln1b_ref, wq_ref, alpha)


def _cross_kernel(q_ref, mk_ref, mv_ref, o_ref):
    for i in range(q_ref.shape[0]):
        o_ref[i] = _cross_heads(q_ref[i], mk_ref.at[i], mv_ref.at[i])


def _out_proj_kernel(o_ref, x1_ref, wo2_ref, ln2g_ref, ln2b_ref, y_ref, *, alpha):
    y_ref[...] = _post_ln(x1_ref[...], _dot(o_ref[...], wo2_ref[...]), ln2g_ref[...], ln2b_ref[...], alpha)


def _mix_cross_split(g, d, x, w_out, ln1g, ln1b, wq, mk, mv, mem_layer, wo, ln2g, ln2b, alpha):
    bsz, r, _ = x.shape
    rows = bsz * r
    flat = lambda a: a.reshape(rows, a.shape[-1])
    full = lambda a: pl.BlockSpec(a.shape, lambda *_: (0,) * a.ndim)
    x1, q = pl.pallas_call(
        functools.partial(_mix_proj_kernel, alpha=alpha),
        grid=(1,),
        in_specs=[full(flat(g)), full(flat(d)), full(flat(x)), full(w_out), full(ln1g), full(ln1b), full(wq)],
        out_specs=[full(flat(x)), full(flat(x))],
        out_shape=[jax.ShapeDtypeStruct((rows, D_MODEL), F32), jax.ShapeDtypeStruct((rows, D_MODEL), BF16)],
        compiler_params=_cparams(("arbitrary",)),
        name="mix_proj",
    )(flat(g), flat(d), flat(x), w_out, ln1g, ln1b, wq)
    sb = math.gcd(bsz, 4)
    seq = pl.BlockSpec((sb, r, D_MODEL), lambda b: (b, 0, 0))
    mem = pl.BlockSpec((None, sb, MEM_LEN * _MEM_SPLIT, LANES), lambda b: (mem_layer, b, 0, 0))
    o = pl.pallas_call(
        _cross_kernel,
        grid=(bsz // sb,),
        in_specs=[seq, mem, mem],
        out_specs=seq,
        out_shape=jax.ShapeDtypeStruct((bsz, r, D_MODEL), BF16),
        compiler_params=_cparams(("parallel",)),
        name="cross",
    )(q.reshape(bsz, r, D_MODEL), mk, mv)
    y = pl.pallas_call(
        functools.partial(_out_proj_kernel, alpha=alpha),
        grid=(1,),
        in_specs=[full(flat(x)), full(flat(x)), full(wo), full(ln2g), full(ln2b)],
        out_specs=full(flat(x)),
        out_shape=jax.ShapeDtypeStruct((rows, D_MODEL), F32),
        compiler_params=_cparams(("arbitrary",)),
        name="out_proj",
    )(flat(o), x1, wo, ln2g, ln2b)
    return y.reshape(bsz, r, D_MODEL)


def _mix_cross(g, d, x, w_out, ln1g, ln1b, wq, mk, mv, mem_layer, wo, ln2g, ln2b, tm, alpha):
    bsz, r, _ = x.shape
    row = lambda n: pl.BlockSpec((None, tm, n), lambda b, i: (b, i, 0))
    full = lambda a: pl.BlockSpec(a.shape, lambda b, i: (0,) * a.ndim)
    mem = pl.BlockSpec((None, None, MEM_LEN * _MEM_SPLIT, LANES), lambda b, i: (mem_layer, b, 0, 0))
    return pl.pallas_call(
        functools.partial(_mix_cross_kernel, alpha=alpha),
        grid=(bsz, r // tm),
        in_specs=[row(GLA_V), row(DIFF_V), row(D_MODEL), full(w_out), full(ln1g), full(ln1b), full(wq),
                  mem, mem, full(wo), full(ln2g), full(ln2b)],
        out_specs=row(D_MODEL),
        out_shape=jax.ShapeDtypeStruct(x.shape, F32),
        compiler_params=_cparams(("parallel", "parallel")),
        name="mix_cross",
    )(g, d, x, w_out, ln1g, ln1b, wq, mk, mv, wo, ln2g, ln2b)


_CARRY = 8
_FF_CHUNK = 256


def _ffn_kernel(x_ref, st_ref, wua_ref, wub_ref, cw_ref, cb_ref, wd_ref, g_ref, b_ref,
                o_ref, ns_ref, carry_scr, h_scr, *, tm, stride, n_valid, alpha):
    i = pl.program_id(1)
    x = x_ref[...]
    xb = x.astype(BF16)
    rows8 = lax.broadcasted_iota(jnp.int32, (_CARRY, 1), 0)

    if stride == 1:
        @pl.when(i == 0)
        def _():
            for half in range(2):
                carry_scr[half, _CARRY - 2:_CARRY - 1, :] = st_ref[half, 0]
                carry_scr[half, _CARRY - 1:_CARRY, :] = st_ref[half, 1]

    for c in range(D_FF // _FF_CHUNK):
        cs = slice(c * _FF_CHUNK, (c + 1) * _FF_CHUNK)
        halves = []
        for half, w_ref in enumerate((wua_ref, wub_ref)):
            u = _dot(xb, w_ref[:, cs])
            if stride == 1:
                p1 = carry_scr[half, _CARRY - 1:_CARRY, cs]
                p2 = carry_scr[half, _CARRY - 2:_CARRY - 1, cs]
                r1, r2 = pltpu.roll(u, 1, 0), pltpu.roll(u, 2, 0)
                u1 = jnp.concatenate([jnp.where(rows8 == 0, p1, r1[0:_CARRY]), r1[_CARRY:]], axis=0)
                u2 = jnp.concatenate(
                    [jnp.where(rows8 == 0, p2, jnp.where(rows8 == 1, p1, r2[0:_CARRY])), r2[_CARRY:]], axis=0)
                carry_scr[half, :, cs] = u[tm - _CARRY:tm, :]
            else:
                padded = jnp.concatenate([st_ref[half, 0, :, cs], st_ref[half, 1, :, cs], u], axis=0)
                u1 = padded[stride:stride + tm]
                u2 = padded[0:tm]
            cw = cw_ref[half, :, cs]
            halves.append(cb_ref[half, :, cs] + cw[0:1] * u2 + cw[1:2] * u1 + cw[2:3] * u)
            for j in range(CONV_W - 1):
                r0 = (n_valid - (CONV_W - 1) + j) * stride
                ns_ref[half, j, :, cs] = u[r0:r0 + stride, :]
        a, bb = halves
        h_scr[:, cs] = (a * (1.0 / (1.0 + jnp.exp(-a))) * bb).astype(BF16)
    o_ref[...] = _post_ln(x, _dot(h_scr[...], wd_ref[...]), g_ref[...], b_ref[...], alpha)


def _ffn(x, state, w_up, conv_w, conv_b, w_down, g, b, tm, stride, n_valid, alpha):
    bsz, r, _ = x.shape
    assert stride == 1 or r == tm
    row = pl.BlockSpec((None, tm, D_MODEL), lambda bi, i: (bi, i, 0))
    st = pl.BlockSpec((None, 2, CONV_W - 1, stride, D_FF), lambda bi, i: (bi, 0, 0, 0, 0))
    once = pl.Buffered(1)
    full = lambda a: pl.BlockSpec(a.shape, lambda bi, i: (0,) * a.ndim, pipeline_mode=once)
    return pl.pallas_call(
        functools.partial(_ffn_kernel, tm=tm, stride=stride, n_valid=n_valid, alpha=alpha),
        grid=(bsz, r // tm),
        in_specs=[row, st,
                  pl.BlockSpec((D_MODEL, D_FF), lambda bi, i: (0, 0), pipeline_mode=once),
                  pl.BlockSpec((D_MODEL, D_FF), lambda bi, i: (0, 1), pipeline_mode=once),
                  full(conv_w), full(conv_b), full(w_down), full(g), full(b)],
        out_specs=[row, pl.BlockSpec((None, None, 2, CONV_W - 1, stride, D_FF), lambda bi, i: (bi, i, 0, 0, 0, 0))],
        out_shape=[jax.ShapeDtypeStruct(x.shape, F32),
                   jax.ShapeDtypeStruct((bsz, r // tm, 2, CONV_W - 1, stride, D_FF), F32)],
        scratch_shapes=[pltpu.VMEM((2, _CARRY, D_FF), F32), pltpu.VMEM((tm, D_FF), BF16)],
        compiler_params=_cparams(("parallel", "arbitrary")),
        name="ffn",
    )(x, state, w_up, w_up, conv_w, conv_b, w_down, g, b)


def _rope_tables(pos):
    half = DIFF_DH // 2
    inv = ROPE_THETA ** (-jnp.arange(half, dtype=F32) / half)
    ang = pos.astype(F32)[:, None] * inv[None, :]
    cos = jnp.tile(jnp.cos(ang), (1, LANES // half))
    sin = jnp.tile(jnp.sin(ang), (1, LANES // half))
    first = (jnp.arange(LANES) % DIFF_DH) < half
    return cos, jnp.where(first, -sin, 0.0), jnp.where(first, 0.0, sin)


def _conv_state_in(s):
    return s.reshape(s.shape[0], CONV_W - 1, 2, D_FF).transpose(2, 1, 0, 3)[None]


def _conv_state_out(s):
    return s[0].transpose(2, 1, 0, 3).reshape(s.shape[3], CONV_W - 1, 2 * D_FF)


def kernel(x_prompt, x_sample, mem_prompt, cache_k, cache_v, page_table, cache_mem_k, cache_mem_v,
           state_gla, state_conv, w_in, gla_wa2, gla_ba, gla_norm_w, diff_lambda, diff_norm_w, w_out,
           ln1_g, ln1_b, cross_wq, cross_wk, cross_wv, cross_wo, ln2_g, ln2_b,
           ffn_w_up, ffn_conv_w, ffn_conv_b, ffn_w_down, ln3_g, ln3_b):
    depth = w_in.shape[0]
    bp, tp, _ = x_prompt.shape
    bs, ts, _ = x_sample.shape
    n_pages = page_table.shape[1]
    past_len = n_pages * PAGE_SIZE
    alpha = (2 * depth) ** 0.25
    assert ts <= _QROWS and tp % 512 == 0

    tm_p = 1024
    tq = 512
    gla_tt = 512
    pages = math.gcd(n_pages, 16)

    tabs_p = _rope_tables(jnp.arange(tp))
    pos_s = past_len + jnp.minimum(jnp.arange(SAMPLE_PAD), ts - 1)
    tabs_s = _rope_tables(jnp.tile(pos_s, bs))

    xp = x_prompt.reshape(bp * tp, D_MODEL)
    xs = jnp.pad(x_sample, ((0, 0), (0, SAMPLE_PAD - ts), (0, 0))).reshape(bs * SAMPLE_PAD, D_MODEL)
    mem2 = mem_prompt.reshape(bp * MEM_LEN, D_MODEL)
    zero_conv = jnp.zeros((bp, 2, CONV_W - 1, 1, D_FF), F32)
    n_phys = cache_k.shape[1]
    cache_k4 = cache_k.reshape(depth, n_phys, PAGE_SIZE * DIFF_HEADS, 2 * DIFF_DH)
    cache_v4 = cache_v.reshape(depth, n_phys, PAGE_SIZE * DIFF_HEADS, DIFF_DV)
    cmem_k = _mem_rows_view(cache_mem_k)
    cmem_v = _mem_rows_view(cache_mem_v)

    outs = [[] for _ in range(10)]
    for l in range(depth):
        lam_init = 0.8 - 0.6 * math.exp(-0.3 * l)
        wl = w_in[l]
        ga0 = 2 * GLA_QK + 2 * GLA_V
        w_in_l = jnp.concatenate(
            [wl[:, :ga0], wl[:, ga0 + GLA_RANK:], wl[:, ga0:ga0 + GLA_RANK],
             jnp.zeros((D_MODEL, LANES - GLA_RANK), F32)], axis=1).astype(BF16)
        wa2_l = jnp.pad(gla_wa2[l], ((0, LANES - GLA_RANK), (0, 0))).astype(BF16)
        ba_l = gla_ba[l].reshape(1, GLA_QK)
        gnw = gla_norm_w[l].reshape(1, GLA_DV)
        dnw = diff_norm_w[l].reshape(1, DIFF_DV)
        w_out_l = w_out[l].astype(BF16)
        wq_l = cross_wq[l].astype(BF16)
        wk_l = cross_wk[l].astype(BF16)
        wv_l = cross_wv[l].astype(BF16)
        wo_l = cross_wo[l].astype(BF16)
        w_up_l = ffn_w_up[l].astype(BF16)
        w_down_l = ffn_w_down[l].astype(BF16)
        cw_l = ffn_conv_w[l].reshape(CONV_W, 2, D_FF).transpose(1, 0, 2)
        cb_l = ffn_conv_b[l].reshape(2, 1, D_FF)
        r1 = lambda a: a[l].reshape(1, D_MODEL)
        ln = [r1(a) for a in (ln1_g, ln1_b, ln2_g, ln2_b, ln3_g, ln3_b)]

        gq, gk, gg, gv, gr, dq, dkf, dvf, dkb, _, dvt = _inproj(xp, w_in_l, wa2_l, ba_l, tabs_p, tp // tq, tq)
        b3 = lambda a: a.reshape(bp, tp, a.shape[-1])
        go, s_p = _gla(b3(gq), b3(gk), b3(gg), b3(gv), b3(gr), gnw, None, bp, gla_tt, GLA_CHUNK, GLA_SUB, None)
        do = _diff_prompt(diff_lambda[l], dnw, b3(dq), b3(dkb),
                          dvt.reshape(bp, tp // tq, DIFF_HEADS, DIFF_DV, tq), tq, lam_init)
        mk_p, mv_p = _memkv(mem2, wk_l, wv_l)
        x2 = _mix_cross(go, do, xp.reshape(bp, tp, D_MODEL), w_out_l, ln[0], ln[1], wq_l,
                        mk_p.reshape(1, bp, MEM_LEN * _MEM_SPLIT, LANES),
                        mv_p.reshape(1, bp, MEM_LEN * _MEM_SPLIT, LANES), 0,
                        wo_l, ln[2], ln[3], tm_p, alpha)
        x3, conv_p = _ffn(x2, zero_conv, w_up_l, cw_l, cb_l, w_down_l, ln[4], ln[5], tm_p, 1, tm_p, alpha)
        xp = x3.reshape(bp * tp, D_MODEL)
        outs[0].append(dkf.reshape(bp, tp // PAGE_SIZE, PAGE_SIZE, DIFF_HEADS, 2 * DIFF_DH))
        outs[1].append(dvf.reshape(bp, tp // PAGE_SIZE, PAGE_SIZE, DIFF_HEADS, DIFF_DV))
        outs[4].append(s_p)
        outs[6].append(conv_p[:, -1].transpose(0, 2, 1, 3, 4).reshape(bp, CONV_W - 1, 2 * D_FF))
        outs[8].append(_mem_rows_unview(mk_p.reshape(bp, MEM_LEN * _MEM_SPLIT, LANES)))
        outs[9].append(_mem_rows_unview(mv_p.reshape(bp, MEM_LEN * _MEM_SPLIT, LANES)))

        gq, gk, gg, gv, gr, dq, dkf, dvf, dkb, dvb, _ = _inproj(xs, w_in_l, wa2_l, ba_l, tabs_s, 1, bs * SAMPLE_PAD)
        s3 = lambda a: a.reshape(bs, SAMPLE_PAD, a.shape[-1])
        go, s_s = _gla(s3(gq), s3(gk), s3(gg), s3(gv), s3(gr), gnw, state_gla[l],
                       math.gcd(bs, 8), SAMPLE_PAD, SAMPLE_PAD, SAMPLE_PAD, ts)
        dq_s = s3(dq)[:, :_QROWS].astype(F32)
        do8 = _diff_sample(page_table, diff_lambda[l], dnw, dq_s, s3(dkb), s3(dvb),
                           cache_k4, cache_v4, l, pages, ts, lam_init)
        do = jnp.pad(do8, ((0, 0), (0, SAMPLE_PAD - _QROWS), (0, 0))).astype(BF16)
        x2 = _mix_cross_split(go, do, xs.reshape(bs, SAMPLE_PAD, D_MODEL), w_out_l, ln[0], ln[1], wq_l,
                              cmem_k, cmem_v, l, wo_l, ln[2], ln[3], alpha)
        x2t = x2.transpose(1, 0, 2).reshape(1, SAMPLE_PAD * bs, D_MODEL)
        x3t, conv_s = _ffn(x2t, _conv_state_in(state_conv[l]), w_up_l, cw_l, cb_l, w_down_l, ln[4], ln[5],
                           SAMPLE_PAD * bs, bs, ts, alpha)
        xs = x3t.reshape(SAMPLE_PAD, bs, D_MODEL).transpose(1, 0, 2).reshape(bs * SAMPLE_PAD, D_MODEL)
        outs[2].append(dkf.reshape(bs, SAMPLE_PAD, DIFF_HEADS, 2 * DIFF_DH)[:, :ts])
        outs[3].append(dvf.reshape(bs, SAMPLE_PAD, DIFF_HEADS, DIFF_DV)[:, :ts])
        outs[5].append(s_s)
        outs[7].append(_conv_state_out(conv_s[:, -1]))

    st = [jnp.stack(o) for o in outs]
    y_prompt = xp.reshape(bp, tp, D_MODEL)
    y_sample = xs.reshape(bs, SAMPLE_PAD, D_MODEL)[:, :ts]
    return (y_prompt, y_sample, st[0], st[1], st[2], st[3], st[4], st[5], st[6], st[7], st[8], st[9])
```
